```python
import math
import jax, jax.numpy as jnp
from jax import lax
import numpy as np

D_MODEL = 1024
BATCH = 8
SEQ = 2048
DEPTH = 4

D_MIX = D_MODEL
A_DIM = D_MIX // 4
A_GROUPS = 4
A_CHUNK = 128
B_HEAD_DIM = 128
B_DIM = D_MIX // 2
B_HEADS = B_DIM // B_HEAD_DIM
B_CONV = 4
B_CHUNK = 64
C_HEAD_DIM = 64
C_DIM = D_MIX - A_DIM - B_DIM
C_HEADS = C_DIM // C_HEAD_DIM
C_CONFIGS = ((128, 1), (512, 4), (2048, 16))
C_QBLOCK = 128
ROPE_THETA = 10000.0
IN_SIZES = (A_DIM, A_DIM, 3 * B_DIM, B_DIM, B_HEADS, B_HEADS, C_DIM, C_DIM, C_DIM)
D_IN = sum(IN_SIZES)
D_FF = ((8 * D_MODEL // 3 + 127) // 128) * 128
N_EXPERTS = 8
TOP_K = 2
N_DENSE = (DEPTH + 1) // 2
N_MOE = DEPTH // 2
DN_ALPHA = (2.0 * DEPTH) ** 0.25
DN_BETA = (8.0 * DEPTH) ** -0.25

kernel_name = "hybrid_gmlp_deltanet_dilated_moe_trunk"


def layer_norm(x, g, b, eps=1e-5):
    xf = x.astype(jnp.float32)
    mu = jnp.mean(xf, axis=-1, keepdims=True)
    var = jnp.mean(jnp.square(xf - mu), axis=-1, keepdims=True)
    return ((xf - mu) * lax.rsqrt(var + eps) * g + b).astype(x.dtype)


def rope(x):
    S, dh = x.shape[1], x.shape[-1]
    inv_freq = ROPE_THETA ** (-jnp.arange(0, dh, 2, dtype=jnp.float32) / dh)
    ang = jnp.arange(S, dtype=jnp.float32)[:, None] * inv_freq[None, :]
    cos = jnp.cos(ang)[None, :, None, :]
    sin = jnp.sin(ang)[None, :, None, :]
    x1 = x[..., : dh // 2].astype(jnp.float32)
    x2 = x[..., dh // 2:].astype(jnp.float32)
    return jnp.concatenate([x1 * cos - x2 * sin, x2 * cos + x1 * sin], axis=-1).astype(x.dtype)


def gmlp_spatial_gate(u, v, ln_g, ln_b, w_s, b_s):
    Bsz, S, _ = u.shape
    v = layer_norm(v, ln_g, ln_b)
    causal = jnp.tril(jnp.ones((A_CHUNK, A_CHUNK), dtype=bool))
    w = jnp.where(causal[None], w_s, jnp.zeros_like(w_s)).astype(v.dtype)
    vc = v.reshape(Bsz, S // A_CHUNK, A_CHUNK, A_GROUPS, A_DIM // A_GROUPS)
    mixed = jnp.einsum('gts,bnsgc->bntgc', w, vc) + b_s.T.astype(v.dtype)[None, None, :, :, None]
    return u * mixed.reshape(Bsz, S, A_DIM)


def causal_dwconv(x, w):
    K = w.shape[0]
    return lax.conv_general_dilated(
        x, w[:, None, :].astype(x.dtype), window_strides=(1,), padding=[(K - 1, 0)],
        dimension_numbers=('NWC', 'WIO', 'NWC'), feature_group_count=x.shape[-1])


def gated_deltanet(q, k, v, z, beta_logit, a_logit, a_log, dt_bias, norm_g):
    f32 = jnp.float32
    Bsz, S, _ = q.shape
    H, dk, C = B_HEADS, B_HEAD_DIM, B_CHUNK
    N = S // C
    q = q.reshape(Bsz, S, H, dk).astype(f32)
    k = k.reshape(Bsz, S, H, dk).astype(f32)
    v = v.reshape(Bsz, S, H, dk).astype(f32)
    q = q * lax.rsqrt(jnp.sum(q * q, -1, keepdims=True) + 1e-6) * (dk ** -0.5)
    k = k * lax.rsqrt(jnp.sum(k * k, -1, keepdims=True) + 1e-6)
    beta = jax.nn.sigmoid(beta_logit.astype(f32))
    g = -jnp.exp(a_log.astype(f32)) * jax.nn.softplus(a_logit.astype(f32) + dt_bias.astype(f32))

    def chunk(t):
        return t.reshape(Bsz, N, C, H, -1).transpose(0, 3, 1, 2, 4)

    qc, kc, vc = chunk(q), chunk(k), chunk(v)
    bc = chunk(beta[..., None])
    gc = jnp.cumsum(chunk(g[..., None])[..., 0], axis=-1)
    lower = jnp.tril(jnp.ones((C, C), dtype=bool))
    strict = jnp.tril(jnp.ones((C, C), dtype=bool), -1)
    decay = jnp.exp(jnp.where(lower, gc[..., :, None] - gc[..., None, :], -jnp.inf))
    k_beta = kc * bc
    m = jnp.where(strict, jnp.einsum('bhnid,bhnjd->bhnij', k_beta, kc) * decay, 0.0)
    eye = jnp.eye(C, dtype=f32)
    rhs = jnp.concatenate([vc * bc, k_beta * jnp.exp(gc)[..., None]], axis=-1)
    sol = lax.linalg.triangular_solve(eye + m, rhs, left_side=True, lower=True, unit_diagonal=True)
    u_c, w_c = sol[..., :dk], sol[..., dk:]
    attn_local = jnp.einsum('bhnid,bhnjd->bhnij', qc, kc) * decay
    q_dec = qc * jnp.exp(gc)[..., None]
    k_dec = kc * jnp.exp(gc[..., -1:] - gc)[..., None]
    g_last = jnp.exp(gc[..., -1])

    def step(state, xs):
        u_i, w_i, a_i, qd_i, kd_i, gl_i = xs
        v_new = u_i - jnp.einsum('bhcd,bhde->bhce', w_i, state)
        o = jnp.einsum('bhcd,bhde->bhce', qd_i, state) + jnp.einsum('bhij,bhje->bhie', a_i, v_new)
        state = state * gl_i[..., None, None] + jnp.einsum('bhcd,bhce->bhde', kd_i, v_new)
        return state, o

    xs = tuple(jnp.moveaxis(t, 2, 0) for t in (u_c, w_c, attn_local, q_dec, k_dec, g_last))
    _, o = lax.scan(step, jnp.zeros((Bsz, H, dk, dk), f32), xs)
    o = o.transpose(1, 0, 3, 2, 4).reshape(Bsz, S, H, dk)
    o = o * lax.rsqrt(jnp.mean(o * o, -1, keepdims=True) + 1e-6) * norm_g.astype(f32)
    o = o * jax.nn.silu(z.reshape(Bsz, S, H, dk).astype(f32))
    return o.reshape(Bsz, S, B_DIM).astype(z.dtype)


def dilated_attention(q, k, v):
    f32 = jnp.float32
    Bsz, S, H, dh = q.shape
    nblk = S // C_QBLOCK
    qb = (q * (dh ** -0.5)).reshape(Bsz, nblk, C_QBLOCK, H, dh).transpose(1, 0, 2, 3, 4)

    def block(args):
        blk, q_blk = args
        t = blk * C_QBLOCK + jnp.arange(C_QBLOCK)
        outs, lses = [], []
        for window, dil in C_CONFIGS:
            j = jnp.arange(window // dil + 1)
            idx = t[:, None] - dil * j[None, :]
            valid = idx >= 0
            idx = jnp.maximum(idx, 0)
            kg = jnp.take(k, idx, axis=1)
            vg = jnp.take(v, idx, axis=1)
            s = jnp.einsum('bqhd,bqjhd->bhqj', q_blk, kg).astype(f32)
            s = jnp.where(valid[None, None], s, -jnp.inf)
            mx = jnp.max(s, axis=-1, keepdims=True)
            e = jnp.exp(s - mx)
            den = jnp.sum(e, axis=-1)
            outs.append(jnp.einsum('bhqj,bqjhd->bqhd', e / den[..., None], vg.astype(f32)))
            lses.append(mx[..., 0] + jnp.log(den))
        wts = jax.nn.softmax(jnp.stack(lses, axis=0), axis=0)
        out = jnp.einsum('cbhq,cbqhd->bqhd', wts, jnp.stack(outs, axis=0))
        return out.astype(q_blk.dtype)

    out = lax.map(block, (jnp.arange(nblk), qb))
    return out.transpose(1, 0, 2, 3, 4).reshape(Bsz, S, H * dh)


def token_mixer(x, w_in, conv_w, a_ln_g, a_ln_b, a_ws, a_bs, b_a_log, b_dt_bias, b_norm_g, w_out):
    Bsz, S, _ = x.shape
    h = x @ w_in
    offs = [int(o) for o in np.cumsum(IN_SIZES)[:-1]]
    a_u, a_v, b_qkv, b_z, b_beta, b_a, c_q, c_k, c_v = jnp.split(h, offs, axis=-1)
    y_a = gmlp_spatial_gate(jax.nn.gelu(a_u), jax.nn.gelu(a_v), a_ln_g, a_ln_b, a_ws, a_bs)
    b_qkv = jax.nn.silu(causal_dwconv(b_qkv, conv_w))
    b_q, b_k, b_v = jnp.split(b_qkv, [B_DIM, 2 * B_DIM], axis=-1)
    y_b = gated_deltanet(b_q, b_k, b_v, b_z, b_beta, b_a, b_a_log, b_dt_bias, b_norm_g)
    cq = rope(c_q.reshape(Bsz, S, C_HEADS, C_HEAD_DIM))
    ck = rope(c_k.reshape(Bsz, S, C_HEADS, C_HEAD_DIM))
    cv = c_v.reshape(Bsz, S, C_HEADS, C_HEAD_DIM)
    y_c = dilated_attention(cq, ck, cv)
    return jnp.concatenate([y_a, y_b, y_c], axis=-1) @ w_out


def swiglu(x, w_gate, w_up, w_down):
    return (jax.nn.silu(x @ w_gate) * (x @ w_up)) @ w_down


def moe_swiglu(x, w_router, w_gate, w_up, w_down):
    Bsz, S, D = x.shape
    xt = x.reshape(Bsz * S, D)
    logits = (xt @ w_router).astype(jnp.float32)
    top_vals, top_idx = lax.top_k(logits, TOP_K)
    probs = jax.nn.softmax(top_vals, axis=-1)
    combine = jnp.sum(jax.nn.one_hot(top_idx, N_EXPERTS, dtype=jnp.float32) * probs[..., None], axis=1)

    def expert(acc, params):
        wg, wu, wd, c = params
        return acc + c[:, None].astype(xt.dtype) * swiglu(xt, wg, wu, wd), None

    acc, _ = lax.scan(expert, jnp.zeros_like(xt), (w_gate, w_up, w_down, combine.T))
    return acc.reshape(Bsz, S, D)


def setup_inputs(seed: int = 0) -> dict:
    key = jax.random.key(seed)
    ks = jax.random.split(key, 24)
    nrm = jax.random.normal
    x = nrm(ks[0], (BATCH, SEQ, D_MODEL), jnp.float32)
    w_in = nrm(ks[1], (DEPTH, D_MODEL, D_IN), jnp.float32) * D_MODEL ** -0.5
    conv_w = nrm(ks[2], (DEPTH, B_CONV, 3 * B_DIM), jnp.float32) * B_CONV ** -0.5
    a_ln_g = 1.0 + 0.02 * nrm(ks[3], (DEPTH, A_DIM), jnp.float32)
    a_ln_b = 0.02 * nrm(ks[4], (DEPTH, A_DIM), jnp.float32)
    a_ws = nrm(ks[5], (DEPTH, A_GROUPS, A_CHUNK, A_CHUNK), jnp.float32) * A_CHUNK ** -0.5
    a_bs = 1.0 + 0.02 * nrm(ks[6], (DEPTH, A_GROUPS, A_CHUNK), jnp.float32)
    b_a_log = jnp.log(jax.random.uniform(ks[7], (DEPTH, B_HEADS), jnp.float32, 1.0, 16.0))
    dt = jnp.exp(jax.random.uniform(ks[8], (DEPTH, B_HEADS), jnp.float32, math.log(1e-3), math.log(1e-1)))
    b_dt_bias = dt + jnp.log(-jnp.expm1(-dt))
    b_norm_g = 1.0 + 0.02 * nrm(ks[9], (DEPTH, B_HEAD_DIM), jnp.float32)
    w_out = nrm(ks[10], (DEPTH, D_MIX, D_MODEL), jnp.float32) * (D_MIX ** -0.5) * DN_BETA
    ln1_g = 1.0 + 0.02 * nrm(ks[11], (DEPTH, D_MODEL), jnp.float32)
    ln1_b = 0.02 * nrm(ks[12], (DEPTH, D_MODEL), jnp.float32)
    ln2_g = 1.0 + 0.02 * nrm(ks[13], (DEPTH, D_MODEL), jnp.float32)
    ln2_b = 0.02 * nrm(ks[14], (DEPTH, D_MODEL), jnp.float32)
    ffn_w_gate = nrm(ks[15], (N_DENSE, D_MODEL, D_FF), jnp.float32) * D_MODEL ** -0.5
    ffn_w_up = nrm(ks[16], (N_DENSE, D_MODEL, D_FF), jnp.float32) * D_MODEL ** -0.5
    ffn_w_down = nrm(ks[17], (N_DENSE, D_FF, D_MODEL), jnp.float32) * (D_FF ** -0.5) * DN_BETA
    moe_router = nrm(ks[18], (N_MOE, D_MODEL, N_EXPERTS), jnp.float32) * D_MODEL ** -0.5
    moe_w_gate = nrm(ks[19], (N_MOE, N_EXPERTS, D_MODEL, D_FF), jnp.float32) * D_MODEL ** -0.5
    moe_w_up = nrm(ks[20], (N_MOE, N_EXPERTS, D_MODEL, D_FF), jnp.float32) * D_MODEL ** -0.5
    moe_w_down = nrm(ks[21], (N_MOE, N_EXPERTS, D_FF, D_MODEL), jnp.float32) * (D_FF ** -0.5) * DN_BETA
    return {"x": x, "w_in": w_in, "conv_w": conv_w, "a_ln_g": a_ln_g, "a_ln_b": a_ln_b,
            "a_ws": a_ws, "a_bs": a_bs, "b_a_log": b_a_log, "b_dt_bias": b_dt_bias,
            "b_norm_g": b_norm_g, "w_out": w_out, "ln1_g": ln1_g, "ln1_b": ln1_b,
            "ln2_g": ln2_g, "ln2_b": ln2_b, "ffn_w_gate": ffn_w_gate, "ffn_w_up": ffn_w_up,
            "ffn_w_down": ffn_w_down, "moe_router": moe_router, "moe_w_gate": moe_w_gate,
            "moe_w_up": moe_w_up, "moe_w_down": moe_w_down}


def reference(x, w_in, conv_w, a_ln_g, a_ln_b, a_ws, a_bs, b_a_log, b_dt_bias, b_norm_g, w_out,
              ln1_g, ln1_b, ln2_g, ln2_b, ffn_w_gate, ffn_w_up, ffn_w_down,
              moe_router, moe_w_gate, moe_w_up, moe_w_down):
    for layer in range(DEPTH):
        mix = token_mixer(x, w_in[layer], conv_w[layer], a_ln_g[layer], a_ln_b[layer], a_ws[layer],
                          a_bs[layer], b_a_log[layer], b_dt_bias[layer], b_norm_g[layer], w_out[layer])
        x = layer_norm(DN_ALPHA * x + mix, ln1_g[layer], ln1_b[layer])
        i = layer // 2
        if layer % 2 == 0:
            f = swiglu(x, ffn_w_gate[i], ffn_w_up[i], ffn_w_down[i])
        else:
            f = moe_swiglu(x, moe_router[i], moe_w_gate[i], moe_w_up[i], moe_w_down[i])
        x = layer_norm(DN_ALPHA * x + f, ln2_g[layer], ln2_b[layer])
    return x
```

```python
import functools
import math

import jax
import jax.numpy as jnp
import numpy as np
from jax import lax
from jax.experimental import pallas as pl
from jax.experimental.pallas import tpu as pltpu

F32 = jnp.float32
BF16 = jnp.bfloat16

D_MODEL = 1024
DEPTH = 4
A_DIM = 256
A_GROUPS = 4
A_CHUNK = 128
B_HEAD_DIM = 128
B_DIM = 512
B_HEADS = 4
B_CONV = 4
B_CHUNK = 64
C_HEAD_DIM = 64
C_DIM = 256
C_HEADS = 4
C_CONFIGS = ((128, 1), (512, 4), (2048, 16))
ROPE_THETA = 10000.0
D_FF = 2816
N_EXPERTS = 8
DN_ALPHA = (2.0 * DEPTH) ** 0.25
LN_EPS = 1e-5

LANES = 128
VMEM_LIMIT = 56 * 1024 * 1024
ROW_TILE = 512
ATT_BLOCK = 128
DN_BLOCK = 2 * B_CHUNK
MOE_TILE = 512
FF_CHUNK = D_FF // 2
DMA_ROWS = 256


def _params(*sem):
    return pltpu.CompilerParams(dimension_semantics=sem, vmem_limit_bytes=VMEM_LIMIT)


def _dot(a, b):
    return jnp.dot(a, b, preferred_element_type=F32)


def _dot_nt(a, b):
    return lax.dot_general(a, b, (((1,), (1,)), ((), ())), preferred_element_type=F32)


def _dot_tn(a, b):
    return lax.dot_general(a, b, (((0,), (0,)), ((), ())), preferred_element_type=F32)


def _split3(x):
    x1 = x.astype(BF16)
    r = x - x1.astype(F32)
    x2 = r.astype(BF16)
    x3 = (r - x2.astype(F32)).astype(BF16)
    return x1, x2, x3


def _layer_norm(y, g, b):
    mu = jnp.mean(y, axis=-1, keepdims=True)
    yc = y - mu
    var = jnp.mean(yc * yc, axis=-1, keepdims=True)
    return yc * lax.rsqrt(var + LN_EPS) * g + b


def _const_spec(shape):
    nd = len(shape)
    return pl.BlockSpec(shape, lambda *_: (0,) * nd)


def _inproj_kernel(x_ref, wa_ref, wb_ref, wz_ref, wc_ref, ws_ref, cos_ref, sin_ref,
                   au_ref, av_ref, bqkv_ref, bz_ref, small_ref, cq_ref, ck_ref, cv_ref):
    x = x_ref[...]
    xb = x.astype(BF16)
    a = jax.nn.gelu(_dot(xb, wa_ref[...]))
    au_ref[...] = a[:, :A_DIM]
    av_ref[...] = a[:, A_DIM:]
    bqkv_ref[...] = _dot(xb, wb_ref[...])
    bz_ref[...] = _dot(xb, wz_ref[...])
    small_ref[...] = jnp.dot(x, ws_ref[...], preferred_element_type=F32,
                             precision=lax.Precision.HIGHEST)
    c = _dot(xb, wc_ref[...])
    cos = cos_ref[...]
    sin = sin_ref[...]
    lane = lax.broadcasted_iota(jnp.int32, (x.shape[0], C_DIM), 1)
    first_half = (lane % C_HEAD_DIM) < (C_HEAD_DIM // 2)
    half = C_HEAD_DIM // 2

    def rope(t):
        swapped = jnp.where(first_half, pltpu.roll(t, C_DIM - half, 1), pltpu.roll(t, half, 1))
        return t * cos + swapped * sin

    cq_ref[...] = (rope(c[:, :C_DIM]) * (C_HEAD_DIM ** -0.5)).astype(BF16)
    ck_ref[...] = rope(c[:, C_DIM:2 * C_DIM]).astype(BF16)
    cv_ref[...] = c[:, 2 * C_DIM:].astype(BF16)


def _inproj(x2d, wa, wb, wz, wc, ws, cos_t, sin_t, seq):
    T = x2d.shape[0]
    tm = ROW_TILE
    spt = seq // tm
    row = lambda n: pl.BlockSpec((tm, n), lambda i: (i, 0))
    pos = pl.BlockSpec((tm, C_DIM), lambda i: (i % spt, 0))
    outs = [(A_DIM, F32), (A_DIM, F32), (3 * B_DIM, F32), (B_DIM, F32), (LANES, F32),
            (C_DIM, BF16), (C_DIM, BF16), (C_DIM, BF16)]
    return pl.pallas_call(
        _inproj_kernel,
        grid=(T // tm,),
        in_specs=[row(D_MODEL), _const_spec(wa.shape), _const_spec(wb.shape), _const_spec(wz.shape),
                  _const_spec(wc.shape), _const_spec(ws.shape), pos, pos],
        out_specs=[row(n) for n, _ in outs],
        out_shape=[jax.ShapeDtypeStruct((T, n), dt) for n, dt in outs],
        compiler_params=_params("parallel"),
        name="inproj",
    )(x2d, wa, wb, wz, wc, ws, cos_t, sin_t)


def _gmlp_kernel(u_ref, v_ref, g_ref, b_ref, ws_ref, bias_ref, o_ref):
    n = u_ref.shape[0] // A_CHUNK
    r = lax.broadcasted_iota(jnp.int32, (A_CHUNK, A_CHUNK), 0)
    c = lax.broadcasted_iota(jnp.int32, (A_CHUNK, A_CHUNK), 1)
    causal = r >= c
    group = lax.broadcasted_iota(jnp.int32, (A_CHUNK, A_DIM), 1) // (A_DIM // A_GROUPS)
    ws = [jnp.where(causal, ws_ref[g], 0.0).astype(BF16) for g in range(A_GROUPS)]
    bias = bias_ref[...]
    for i in range(n):
        rows = pl.ds(i * A_CHUNK, A_CHUNK)
        vn = _layer_norm(v_ref[rows, :], g_ref[...], b_ref[...]).astype(BF16)
        mixed = bias
        for g in range(A_GROUPS):
            mixed = mixed + jnp.where(group == g, _dot(ws[g], vn), 0.0)
        o_ref[rows, :] = (u_ref[rows, :] * mixed).astype(o_ref.dtype)


def _gmlp(au, av, ln_g, ln_b, ws, bias2d):
    T = au.shape[0]
    tm = ROW_TILE
    row = pl.BlockSpec((tm, A_DIM), lambda i: (i, 0))
    return pl.pallas_call(
        _gmlp_kernel,
        grid=(T // tm,),
        in_specs=[row, row, _const_spec(ln_g.shape), _const_spec(ln_b.shape),
                  _const_spec(ws.shape), _const_spec(bias2d.shape)],
        out_specs=row,
        out_shape=jax.ShapeDtypeStruct((T, A_DIM), BF16),
        compiler_params=_params("parallel"),
        name="gmlp",
    )(au, av, ln_g, ln_b, ws, bias2d)


def _unit_lower_inverse_minus_eye(m, blk):
    def bf(t):
        return t.astype(BF16)

    d16 = jnp.where(blk(16), m, 0.0)
    x = -d16
    p = x
    y = _dot(bf(x), bf(x))
    for step in range(3):
        p = p + y + _dot(bf(p), bf(y))
        if step < 2:
            y = _dot(bf(y), bf(y))
    for size in (32, 64):
        l = jnp.where(blk(size) & jnp.logical_not(blk(size // 2)), m, 0.0)
        q = l + _dot(bf(p), bf(l))
        p = p - (q + _dot(bf(q), bf(p)))
    return p


def _deltanet_kernel(q_ref, k_ref, v_ref, z_ref, s_ref, cw_ref, alog_ref, dtb_ref, ng_ref, o_ref,
                     state_ref, *, heads):
    S = q_ref.shape[0]
    nblk = S // DN_BLOCK
    dk = B_HEAD_DIM
    C = B_CHUNK
    hp = pl.program_id(1)

    r = lax.broadcasted_iota(jnp.int32, (DN_BLOCK, DN_BLOCK), 0)
    c = lax.broadcasted_iota(jnp.int32, (DN_BLOCK, DN_BLOCK), 1)

    def blk(size):
        return (r // size) == (c // size)

    lower = (r >= c) & blk(C)
    strict = (r > c) & blk(C)
    cum_mask = jnp.where(lower, 1.0, 0.0).astype(BF16)
    first_chunk = lax.broadcasted_iota(jnp.int32, (DN_BLOCK, dk), 0) < C
    lane = lax.broadcasted_iota(jnp.int32, (DN_BLOCK, LANES), 1)

    state_ref[...] = jnp.zeros_like(state_ref)

    def conv_silu(ref, w, r0, not_first):
        cur = ref[pl.ds(r0, DN_BLOCK), :]
        prev = ref[pl.ds(pl.multiple_of(jnp.maximum(r0 - 8, 0), 8), 8), :] * not_first
        ext = jnp.concatenate([prev, cur], axis=0)
        acc = cur * w[B_CONV - 1:B_CONV, :]
        for s in range(1, B_CONV):
            acc = acc + pltpu.roll(ext, s, 0)[8:, :] * w[B_CONV - 1 - s:B_CONV - s, :]
        return acc * jax.nn.sigmoid(acc)

    def body(n, carry):
        r0 = pl.multiple_of(n * DN_BLOCK, DN_BLOCK)
        not_first = jnp.where(n > 0, 1.0, 0.0).astype(F32)
        q2 = conv_silu(q_ref, cw_ref[0], r0, not_first)
        k2 = conv_silu(k_ref, cw_ref[1], r0, not_first)
        v2 = conv_silu(v_ref, cw_ref[2], r0, not_first)
        z2 = z_ref[pl.ds(r0, DN_BLOCK), :]
        small = s_ref[pl.ds(r0, DN_BLOCK), :]
        sig = jax.nn.sigmoid(small)
        g_all = -jnp.exp(alog_ref[...]) * jax.nn.softplus(small + dtb_ref[...])
        outs = []
        for h in range(heads):
            cols = slice(h * dk, (h + 1) * dk)
            q, k, v = q2[:, cols], k2[:, cols], v2[:, cols]
            q = q * lax.rsqrt(jnp.sum(q * q, -1, keepdims=True) + 1e-6) * (dk ** -0.5)
            k = k * lax.rsqrt(jnp.sum(k * k, -1, keepdims=True) + 1e-6)
            hh = hp * heads + h
            beta = jnp.sum(jnp.where(lane == hh, sig, 0.0), -1, keepdims=True)
            g = jnp.sum(jnp.where(lane == B_HEADS + hh, g_all, 0.0), -1, keepdims=True)
            gb = jnp.broadcast_to(g, (DN_BLOCK, dk))
            g1, g2, g3 = _split3(gb)
            gcol = _dot(cum_mask, g1) + _dot(cum_mask, g2) + _dot(cum_mask, g3)
            grow = gcol.T
            glast = jnp.where(first_chunk, gcol[C - 1:C, :], gcol[2 * C - 1:2 * C, :])
            e_gc = jnp.exp(gcol)
            decay = jnp.exp(jnp.where(lower, gcol - grow, -jnp.inf))
            kb = k * beta
            kbf = k.astype(BF16)
            kk = _dot_nt(kb.astype(BF16), kbf)
            qk = _dot_nt(q.astype(BF16), kbf)
            m = jnp.where(strict, kk * decay, 0.0)
            attn = jnp.where(lower, qk * decay, 0.0)
            t_off = _unit_lower_inverse_minus_eye(m, blk)
            rhs = jnp.concatenate([v * beta, kb * e_gc], axis=1)
            sol = rhs + _dot(t_off.astype(BF16), rhs.astype(BF16))
            u, w = sol[:, :dk], sol[:, dk:]
            qd = q * e_gc
            kd = k * jnp.exp(glast - gcol)
            gl = jnp.exp(glast)
            state = state_ref[h]
            o_parts = []
            for ci in range(2):
                rs = slice(ci * C, (ci + 1) * C)
                wq = jnp.concatenate([w[rs], qd[rs]], axis=0).astype(BF16)
                ws_qs = _dot(wq, state.astype(BF16))
                v_new = u[rs] - ws_qs[:C]
                a_c = attn[rs, ci * C:(ci + 1) * C]
                o_parts.append(ws_qs[C:] + _dot(a_c.astype(BF16), v_new.astype(BF16)))
                state = state * gl[ci * C:ci * C + 1, :] + _dot_tn(kd[rs].astype(BF16), v_new.astype(BF16))
            state_ref[h] = state
            o = jnp.concatenate(o_parts, axis=0)
            o = o * lax.rsqrt(jnp.mean(o * o, -1, keepdims=True) + 1e-6) * ng_ref[...]
            zz = z2[:, cols]
            outs.append(o * (zz * jax.nn.sigmoid(zz)))
        o_ref[pl.ds(r0, DN_BLOCK), :] = jnp.concatenate(outs, axis=1).astype(o_ref.dtype)
        return carry

    lax.fori_loop(0, nblk, body, 0)


def _deltanet(bqkv, bz, small, conv_w3, alog_row, dtb_row, norm_g, batch, seq):
    heads = 2
    wcols = heads * B_HEAD_DIM
    groups = B_HEADS // heads
    nq = B_DIM // wcols

    def qkv_spec(part):
        return pl.BlockSpec((seq, wcols), lambda b, h: (b, part * nq + h))

    return pl.pallas_call(
        functools.partial(_deltanet_kernel, heads=heads),
        grid=(batch, groups),
        in_specs=[qkv_spec(0), qkv_spec(1), qkv_spec(2),
                  pl.BlockSpec((seq, wcols), lambda b, h: (b, h)),
                  pl.BlockSpec((seq, LANES), lambda b, h: (b, 0)),
                  pl.BlockSpec((3, B_CONV, wcols), lambda b, h: (0, 0, h)),
                  _const_spec(alog_row.shape), _const_spec(dtb_row.shape), _const_spec(norm_g.shape)],
        out_specs=pl.BlockSpec((seq, wcols), lambda b, h: (b, h)),
        out_shape=jax.ShapeDtypeStruct((batch * seq, B_DIM), BF16),
        scratch_shapes=[pltpu.VMEM((heads, B_HEAD_DIM, B_HEAD_DIM), F32)],
        compiler_params=_params("parallel", "parallel"),
        name="deltanet",
    )(bqkv, bqkv, bqkv, bz, small, conv_w3, alog_row, dtb_row, norm_g)


def _attn_kernel(q_ref, k_ref, v_ref, bias_ref, o_ref):
    S = q_ref.shape[0]
    nq = S // ATT_BLOCK
    head_of_lane = lax.broadcasted_iota(jnp.int32, (ATT_BLOCK, C_DIM), 1) // C_HEAD_DIM
    for i in range(nq):
        nk = (i + 1) * ATT_BLOCK
        q = q_ref[pl.ds(i * ATT_BLOCK, ATT_BLOCK), :]
        k = k_ref[pl.ds(0, nk), :]
        v = v_ref[pl.ds(0, nk), :]
        bias = bias_ref[:, pl.ds(S - nk, nk)]

        def head(h, out):
            sel = head_of_lane == h
            s = _dot_nt(jnp.where(sel, q, jnp.zeros_like(q)), k) + bias
            mx = jnp.max(s, axis=-1, keepdims=True)
            e = jnp.exp(s - mx)
            den = jnp.sum(e, axis=-1, keepdims=True)
            pv = _dot(e.astype(BF16), v) / den
            return jnp.where(sel, pv, out)

        out = lax.fori_loop(0, C_HEADS, head, jnp.zeros((ATT_BLOCK, C_DIM), F32))
        o_ref[pl.ds(i * ATT_BLOCK, ATT_BLOCK), :] = out.astype(o_ref.dtype)


def _attn_bias(seq):
    r = np.arange(ATT_BLOCK)[:, None]
    c = np.arange(seq)[None, :]
    dist = (seq - ATT_BLOCK) + r - c
    mult = np.zeros(dist.shape, np.float64)
    for window, dil in C_CONFIGS:
        mult += (dist >= 0) & (dist <= window) & (dist % dil == 0)
    with np.errstate(divide="ignore"):
        return jnp.asarray(np.log(mult), F32)


def _attention(cq, ck, cv, bias, batch, seq):
    spec = pl.BlockSpec((seq, C_DIM), lambda b: (b, 0))
    return pl.pallas_call(
        _attn_kernel,
        grid=(batch,),
        in_specs=[spec, spec, spec, _const_spec(bias.shape)],
        out_specs=spec,
        out_shape=jax.ShapeDtypeStruct((batch * seq, C_DIM), BF16),
        compiler_params=_params("parallel"),
        name="dilated_attn",
    )(cq, ck, cv, bias)


def _outproj_kernel(x_ref, ya_ref, yb_ref, yc_ref, wa_ref, wb_ref, wc_ref, g_ref, b_ref, o_ref):
    mix = _dot(ya_ref[...], wa_ref[...]) + _dot(yb_ref[...], wb_ref[...]) + _dot(yc_ref[...], wc_ref[...])
    o_ref[...] = _layer_norm(DN_ALPHA * x_ref[...] + mix, g_ref[...], b_ref[...])


def _outproj(x2d, ya, yb, yc, wa, wb, wc, g, b):
    T = x2d.shape[0]
    tm = ROW_TILE
    row = lambda n: pl.BlockSpec((tm, n), lambda i: (i, 0))
    return pl.pallas_call(
        _outproj_kernel,
        grid=(T // tm,),
        in_specs=[row(D_MODEL), row(A_DIM), row(B_DIM), row(C_DIM), _const_spec(wa.shape),
                  _const_spec(wb.shape), _const_spec(wc.shape), _const_spec(g.shape), _const_spec(b.shape)],
        out_specs=row(D_MODEL),
        out_shape=jax.ShapeDtypeStruct((T, D_MODEL), F32),
        compiler_params=_params("parallel"),
        name="outproj_ln",
    )(x2d, ya, yb, yc, wa, wb, wc, g, b)


def _swiglu_partial(xb, wg, wu, wd):
    gate = _dot(xb, wg)
    up = _dot(xb, wu)
    h = (gate * jax.nn.sigmoid(gate) * up).astype(BF16)
    return _dot(h, wd)


def _ffn_kernel(x_ref, wg_ref, wu_ref, wd_ref, g_ref, b_ref, o_ref, acc_ref):
    j = pl.program_id(1)
    part = _swiglu_partial(x_ref[...].astype(BF16), wg_ref[...], wu_ref[...], wd_ref[...])

    @pl.when(j == 0)
    def _():
        acc_ref[...] = part

    @pl.when(j > 0)
    def _():
        acc_ref[...] += part

    @pl.when(j == pl.num_programs(1) - 1)
    def _():
        o_ref[...] = _layer_norm(DN_ALPHA * x_ref[...] + acc_ref[...], g_ref[...], b_ref[...])


def _ffn(x2d, wg, wu, wd, g, b):
    T = x2d.shape[0]
    tm = ROW_TILE
    tf = FF_CHUNK
    return pl.pallas_call(
        _ffn_kernel,
        grid=(T // tm, D_FF // tf),
        in_specs=[pl.BlockSpec((tm, D_MODEL), lambda i, j: (i, 0)),
                  pl.BlockSpec((D_MODEL, tf), lambda i, j: (0, j)),
                  pl.BlockSpec((D_MODEL, tf), lambda i, j: (0, j)),
                  pl.BlockSpec((tf, D_MODEL), lambda i, j: (j, 0)),
                  _const_spec(g.shape), _const_spec(b.shape)],
        out_specs=pl.BlockSpec((tm, D_MODEL), lambda i, j: (i, 0)),
        out_shape=jax.ShapeDtypeStruct((T, D_MODEL), F32),
        scratch_shapes=[pltpu.VMEM((tm, D_MODEL), F32)],
        compiler_params=_params("parallel", "arbitrary"),
        name="ffn_ln",
    )(x2d, wg, wu, wd, g, b)


def _router_kernel(x_ref, w_ref, route_ref, counts_ref, carry_ref):
    i = pl.program_id(0)
    tm = x_ref.shape[0]

    @pl.when(i == 0)
    def _():
        carry_ref[...] = jnp.zeros_like(carry_ref)

    logits = jnp.dot(x_ref[...], w_ref[...], preferred_element_type=F32, precision=lax.Precision.HIGHEST)
    lane = lax.broadcasted_iota(jnp.int32, (tm, LANES), 1)
    logits = jnp.where(lane < N_EXPERTS, logits, -jnp.inf)
    m1 = jnp.max(logits, axis=-1, keepdims=True)
    e1 = jnp.min(jnp.where(logits == m1, lane, LANES), axis=-1, keepdims=True)
    rest = jnp.where(lane == e1, -jnp.inf, logits)
    m2 = jnp.max(rest, axis=-1, keepdims=True)
    e2 = jnp.min(jnp.where(rest == m2, lane, LANES), axis=-1, keepdims=True)
    t = jnp.exp(m2 - m1)
    p1 = 1.0 / (1.0 + t)
    p2 = t / (1.0 + t)
    hot1 = lane == e1
    hot2 = lane == e2
    hot = jnp.where(hot1 | hot2, 1.0, 0.0)
    rr = lax.broadcasted_iota(jnp.int32, (tm, tm), 0)
    cc = lax.broadcasted_iota(jnp.int32, (tm, tm), 1)
    before = jnp.where(rr > cc, 1.0, 0.0).astype(BF16)
    cnt = _dot(before, hot.astype(BF16)) + carry_ref[0:1, :]
    rank1 = jnp.sum(jnp.where(hot1, cnt, 0.0), axis=-1, keepdims=True)
    rank2 = jnp.sum(jnp.where(hot2, cnt, 0.0), axis=-1, keepdims=True)
    route = jnp.where(lane == 0, e1.astype(F32), 0.0)
    route = jnp.where(lane == 1, e2.astype(F32), route)
    route = jnp.where(lane == 2, p1, route)
    route = jnp.where(lane == 3, p2, route)
    route = jnp.where(lane == 4, rank1, route)
    route = jnp.where(lane == 5, rank2, route)
    route_ref[...] = route
    carry_ref[...] = carry_ref[...] + jnp.sum(hot, axis=0, keepdims=True)
    counts_ref[...] = carry_ref[...]


def _router(x2d, w_pad):
    T = x2d.shape[0]
    tm = ROW_TILE
    return pl.pallas_call(
        _router_kernel,
        grid=(T // tm,),
        in_specs=[pl.BlockSpec((tm, D_MODEL), lambda i: (i, 0)), _const_spec(w_pad.shape)],
        out_specs=[pl.BlockSpec((tm, LANES), lambda i: (i, 0)), _const_spec((8, LANES))],
        out_shape=[jax.ShapeDtypeStruct((T, LANES), F32), jax.ShapeDtypeStruct((8, LANES), F32)],
        scratch_shapes=[pltpu.VMEM((8, LANES), F32)],
        compiler_params=_params("arbitrary"),
        name="moe_router",
    )(x2d, w_pad)


def _dispatch_kernel(pos0_ref, pos1_ref, x_ref, init_ref, xs_ref, sem):
    del init_ref
    n = x_ref.shape[0]

    def row_copy(t, pos_ref):
        return pltpu.make_async_copy(x_ref.at[pl.ds(t, 1), :], xs_ref.at[pl.ds(pos_ref[0, 0, t], 1), :], sem)

    def start(t, c):
        row_copy(t, pos0_ref).start()
        row_copy(t, pos1_ref).start()
        return c

    def wait(t, c):
        row_copy(t, pos0_ref).wait()
        row_copy(t, pos1_ref).wait()
        return c

    lax.fori_loop(0, n, start, 0)
    lax.fori_loop(0, n, wait, 0)


def _dispatch(x2d, pos0, pos1, n_rows):
    T = x2d.shape[0]
    td = DMA_ROWS
    idx = pl.BlockSpec((1, 1, td), lambda i: (i, 0, 0), memory_space=pltpu.SMEM)
    init = jnp.zeros((n_rows, D_MODEL), F32)
    return pl.pallas_call(
        _dispatch_kernel,
        grid=(T // td,),
        in_specs=[idx, idx, pl.BlockSpec((td, D_MODEL), lambda i: (i, 0)),
                  pl.BlockSpec(memory_space=pl.ANY)],
        out_specs=pl.BlockSpec(memory_space=pl.ANY),
        out_shape=jax.ShapeDtypeStruct((n_rows, D_MODEL), F32),
        scratch_shapes=[pltpu.SemaphoreType.DMA(())],
        input_output_aliases={3: 0},
        compiler_params=_params("arbitrary"),
        name="moe_dispatch",
    )(pos0.reshape(T // td, 1, td), pos1.reshape(T // td, 1, td), x2d, init)


def _combine_kernel(pos0_ref, pos1_ref, x_ref, route_ref, ys_ref, g_ref, b_ref, o_ref, buf0, buf1, sem):
    n = x_ref.shape[0]

    def row_copy(t, pos_ref, buf):
        return pltpu.make_async_copy(ys_ref.at[pl.ds(pos_ref[0, 0, t], 1), :], buf.at[pl.ds(t, 1), :], sem)

    def start(t, c):
        row_copy(t, pos0_ref, buf0).start()
        row_copy(t, pos1_ref, buf1).start()
        return c

    def wait(t, c):
        row_copy(t, pos0_ref, buf0).wait()
        row_copy(t, pos1_ref, buf1).wait()
        return c

    lax.fori_loop(0, n, start, 0)
    lax.fori_loop(0, n, wait, 0)
    route = route_ref[...]
    f = route[:, 2:3] * buf0[...] + route[:, 3:4] * buf1[...]
    o_ref[...] = _layer_norm(DN_ALPHA * x_ref[...] + f, g_ref[...], b_ref[...])


def _combine(x2d, route, ys, pos0, pos1, g, b):
    T = x2d.shape[0]
    td = DMA_ROWS
    idx = pl.BlockSpec((1, 1, td), lambda i: (i, 0, 0), memory_space=pltpu.SMEM)
    return pl.pallas_call(
        _combine_kernel,
        grid=(T // td,),
        in_specs=[idx, idx, pl.BlockSpec((td, D_MODEL), lambda i: (i, 0)),
                  pl.BlockSpec((td, LANES), lambda i: (i, 0)),
                  pl.BlockSpec(memory_space=pl.ANY), _const_spec(g.shape), _const_spec(b.shape)],
        out_specs=pl.BlockSpec((td, D_MODEL), lambda i: (i, 0)),
        out_shape=jax.ShapeDtypeStruct((T, D_MODEL), F32),
        scratch_shapes=[pltpu.VMEM((td, D_MODEL), F32), pltpu.VMEM((td, D_MODEL), F32),
                        pltpu.SemaphoreType.DMA(())],
        compiler_params=_params("arbitrary"),
        name="moe_combine_ln",
    )(pos0.reshape(T // td, 1, td), pos1.reshape(T // td, 1, td), x2d, route, ys, g, b)


def _moe_ffn_kernel(te_ref, nt_ref, x_ref, wg_ref, wu_ref, wd_ref, o_ref, acc_ref):
    i = pl.program_id(0)
    j = pl.program_id(1)

    @pl.when(i < nt_ref[0])
    def _():
        part = _swiglu_partial(x_ref[...].astype(BF16), wg_ref[0], wu_ref[0], wd_ref[0])

        @pl.when(j == 0)
        def _():
            acc_ref[...] = part

        @pl.when(j > 0)
        def _():
            acc_ref[...] += part

        @pl.when(j == pl.num_programs(1) - 1)
        def _():
            o_ref[...] = acc_ref[...]

    @pl.when((i >= nt_ref[0]) & (j == pl.num_programs(1) - 1))
    def _():
        o_ref[...] = jnp.zeros_like(o_ref)


def _moe_ffn(xs, tile_expert, n_tiles, wg, wu, wd):
    n_rows = xs.shape[0]
    tm = MOE_TILE
    tf = FF_CHUNK
    nf = D_FF // tf

    def row_map(i, j, te, nt):
        return (jnp.minimum(i, nt[0] - 1), 0)

    def ee(i, te, nt):
        return te[jnp.minimum(i, nt[0] - 1)]

    def jj(i, j, nt):
        return jnp.where(i < nt[0], j, nf - 1)

    grid_spec = pltpu.PrefetchScalarGridSpec(
        num_scalar_prefetch=2,
        grid=(n_rows // tm, nf),
        in_specs=[pl.BlockSpec((tm, D_MODEL), row_map),
                  pl.BlockSpec((1, D_MODEL, tf), lambda i, j, te, nt: (ee(i, te, nt), 0, jj(i, j, nt))),
                  pl.BlockSpec((1, D_MODEL, tf), lambda i, j, te, nt: (ee(i, te, nt), 0, jj(i, j, nt))),
                  pl.BlockSpec((1, tf, D_MODEL), lambda i, j, te, nt: (ee(i, te, nt), jj(i, j, nt), 0))],
        out_specs=pl.BlockSpec((tm, D_MODEL), lambda i, j, te, nt: (i, 0)),
        scratch_shapes=[pltpu.VMEM((tm, D_MODEL), F32)],
    )
    return pl.pallas_call(
        _moe_ffn_kernel,
        grid_spec=grid_spec,
        out_shape=jax.ShapeDtypeStruct((n_rows, D_MODEL), F32),
        compiler_params=_params("arbitrary", "arbitrary"),
        name="moe_ffn",
    )(tile_expert, n_tiles, xs, wg, wu, wd)


def _moe(x2d, w_router, wg, wu, wd, g, b):
    T = x2d.shape[0]
    tm = MOE_TILE
    n_tiles_max = (2 * T) // tm + N_EXPERTS
    n_rows = n_tiles_max * tm
    w_pad = jnp.zeros((D_MODEL, LANES), F32).at[:, :N_EXPERTS].set(w_router)
    route, counts = _router(x2d, w_pad)
    cnt = counts[0, :N_EXPERTS].astype(jnp.int32)
    padded = ((cnt + tm - 1) // tm) * tm
    ends = jnp.cumsum(padded)
    offs = ends - padded
    e = route[:, 0:2].astype(jnp.int32)
    rank = route[:, 4:6].astype(jnp.int32)
    pos = offs[e] + rank
    pos0, pos1 = pos[:, 0], pos[:, 1]
    tile_start = jnp.arange(n_tiles_max, dtype=jnp.int32) * tm
    tile_expert = jnp.minimum(jnp.sum(tile_start[:, None] >= ends[None, :], axis=1), N_EXPERTS - 1).astype(jnp.int32)
    n_tiles = (ends[-1:] // tm).astype(jnp.int32)
    xs = _dispatch(x2d, pos0, pos1, n_rows)
    ys = _moe_ffn(xs, tile_expert, n_tiles, wg, wu, wd)
    return _combine(x2d, route, ys, pos0, pos1, g, b)


def _rope_tables(seq):
    inv_freq = ROPE_THETA ** (-jnp.arange(0, C_HEAD_DIM, 2, dtype=F32) / C_HEAD_DIM)
    ang = jnp.arange(seq, dtype=F32)[:, None] * inv_freq[None, :]
    cos, sin = jnp.cos(ang), jnp.sin(ang)
    cos_h = jnp.concatenate([cos, cos], axis=-1)
    sin_h = jnp.concatenate([-sin, sin], axis=-1)
    return jnp.tile(cos_h, (1, C_HEADS)), jnp.tile(sin_h, (1, C_HEADS))


def kernel(x, w_in, conv_w, a_ln_g, a_ln_b, a_ws, a_bs, b_a_log, b_dt_bias, b_norm_g, w_out, ln1_g, ln1_b, ln2_g, ln2_b, ffn_w_gate, ffn_w_up, ffn_w_down, moe_router, moe_w_gate, moe_w_up, moe_w_down):
    batch, seq, _ = x.shape
    T = batch * seq
    cos_t, sin_t = _rope_tables(seq)
    att_bias = _attn_bias(seq)
    o_a, o_bq, o_bz, o_beta, o_cq = 0, 2 * A_DIM, 2 * A_DIM + 3 * B_DIM, 2 * A_DIM + 4 * B_DIM, 2 * A_DIM + 4 * B_DIM + 2 * B_HEADS
    row = lambda v: v.reshape(1, -1)
    h2d = x.reshape(T, D_MODEL)
    for layer in range(DEPTH):
        w = w_in[layer]
        wa = w[:, o_a:o_bq].astype(BF16)
        wb = w[:, o_bq:o_bz].astype(BF16)
        wz = w[:, o_bz:o_beta].astype(BF16)
        ws = jnp.zeros((D_MODEL, LANES), F32).at[:, :2 * B_HEADS].set(w[:, o_beta:o_cq])
        wc = w[:, o_cq:].astype(BF16)
        au, av, bqkv, bz, small, cq, ck, cv = _inproj(h2d, wa, wb, wz, wc, ws, cos_t, sin_t, seq)

        bias2d = jnp.repeat(a_bs[layer].T, A_DIM // A_GROUPS, axis=1)
        ya = _gmlp(au, av, row(a_ln_g[layer]), row(a_ln_b[layer]), a_ws[layer], bias2d)

        zeros_row = jnp.zeros((1, LANES), F32)
        alog_row = zeros_row.at[0, B_HEADS:2 * B_HEADS].set(b_a_log[layer])
        dtb_row = zeros_row.at[0, B_HEADS:2 * B_HEADS].set(b_dt_bias[layer])
        conv_w3 = conv_w[layer].reshape(B_CONV, 3, B_DIM).transpose(1, 0, 2)
        yb = _deltanet(bqkv, bz, small, conv_w3, alog_row, dtb_row, row(b_norm_g[layer]), batch, seq)

        yc = _attention(cq, ck, cv, att_bias, batch, seq)

        wo = w_out[layer].astype(BF16)
        h2d = _outproj(h2d, ya, yb, yc, wo[:A_DIM], wo[A_DIM:A_DIM + B_DIM], wo[A_DIM + B_DIM:],
                       row(ln1_g[layer]), row(ln1_b[layer]))
        i = layer // 2
        if layer % 2 == 0:
            h2d = _ffn(h2d, ffn_w_gate[i].astype(BF16), ffn_w_up[i].astype(BF16), ffn_w_down[i].astype(BF16),
                       row(ln2_g[layer]), row(ln2_b[layer]))
        else:
            h2d = _moe(h2d, moe_router[i], moe_w_gate[i].astype(BF16), moe_w_up[i].astype(BF16),
                       moe_w_down[i].astype(BF16), row(ln2_g[layer]), row(ln2_b[layer]))
    return h2d.reshape(batch, seq, D_MODEL)
```

```python
import functools

import jax
import jax.numpy as jnp
import numpy as np
from jax import lax
from jax.experimental import pallas as pl
from jax.experimental.pallas import tpu as pltpu

F32 = jnp.float32
BF16 = jnp.bfloat16

D_MODEL = 1024
DEPTH = 4
A_DIM = 256
A_GROUPS = 4
A_CHUNK = 128
B_HEAD_DIM = 128
B_DIM = 512
B_HEADS = 4
B_CONV = 4
B_CHUNK = 64
C_HEAD_DIM = 64
C_DIM = 256
C_HEADS = 4
C_CONFIGS = ((128, 1), (512, 4), (2048, 16))
ROPE_THETA = 10000.0
D_FF = 2816
N_EXPERTS = 8
DN_ALPHA = (2.0 * DEPTH) ** 0.25
LN_EPS = 1e-5

LANES = 128
VMEM_LIMIT = 56 * 1024 * 1024
ROW_TILE = 512
ATT_BLOCK = 128
ATT_KEYS = 256
DN_BLOCK = 2 * B_CHUNK
MOE_TILE = 512
FF_CHUNK = D_FF // 2
DMA_ROWS = 256
DMA_UNROLL = 8


def _params(*sem):
    return pltpu.CompilerParams(dimension_semantics=sem, vmem_limit_bytes=VMEM_LIMIT)


def _dot(a, b):
    return jnp.dot(a, b, preferred_element_type=F32)


def _dot_nt(a, b):
    return lax.dot_general(a, b, (((1,), (1,)), ((), ())), preferred_element_type=F32)


def _split3(x):
    x1 = x.astype(BF16)
    r = x - x1.astype(F32)
    x2 = r.astype(BF16)
    x3 = (r - x2.astype(F32)).astype(BF16)
    return x1, x2, x3


NARROW = 8


def _pack_narrow(w):
    pieces = _split3(w)
    packed = jnp.zeros((w.shape[0], LANES), BF16)
    for i, piece in enumerate(pieces):
        packed = packed.at[:, i * NARROW:(i + 1) * NARROW].set(piece)
    return packed


def _dot_narrow(x, w_packed):
    x1 = x.astype(BF16)
    x2 = (x - x1.astype(F32)).astype(BF16)
    p = _dot(x1, w_packed) + _dot(x2, w_packed)
    p = p + pltpu.roll(p, LANES - NARROW, 1) + pltpu.roll(p, LANES - 2 * NARROW, 1)
    lane = lax.broadcasted_iota(jnp.int32, p.shape, 1)
    return jnp.where(lane < NARROW, p, 0.0)


def _layer_norm(y, g, b):
    mu = jnp.mean(y, axis=-1, keepdims=True)
    yc = y - mu
    var = jnp.mean(yc * yc, axis=-1, keepdims=True)
    return yc * lax.rsqrt(var + LN_EPS) * g + b


def _const_spec(shape):
    nd = len(shape)
    return pl.BlockSpec(shape, lambda *_: (0,) * nd)


def _inproj_kernel(x_ref, xp_ref, wa_ref, wb_ref, wz_ref, wc_ref, ws_ref, cw_ref, cos_ref, sin_ref,
                   au_ref, av_ref, bq_ref, bk_ref, bv_ref, bz_ref, small_ref, cq_ref, ck_ref, cv_ref, *, spt):
    tm = x_ref.shape[0]
    x = x_ref[...]
    xb = x.astype(BF16)
    a = jax.nn.gelu(_dot(xb, wa_ref[...]))
    au_ref[...] = a[:, :A_DIM]
    av_ref[...] = a[:, A_DIM:]

    seq_start = (pl.program_id(0) % spt) == 0
    xpb = jnp.where(seq_start, 0.0, xp_ref[...]).astype(BF16)
    dk = B_HEAD_DIM
    for part, out_ref in enumerate((bq_ref, bk_ref, bv_ref)):
        cols = slice(part * B_DIM, (part + 1) * B_DIM)
        w = wb_ref[:, cols]
        cur = _dot(xb, w)
        ext = jnp.concatenate([_dot(xpb, w), cur], axis=0)
        cw = cw_ref[:, cols]
        acc = cur * cw[B_CONV - 1:B_CONV, :]
        for s in range(1, B_CONV):
            acc = acc + pltpu.roll(ext, s, 0)[8:, :] * cw[B_CONV - 1 - s:B_CONV - s, :]
        y = acc * jax.nn.sigmoid(acc)
        if part < 2:
            scale = dk ** -0.5 if part == 0 else 1.0
            segs = []
            for h in range(B_HEADS):
                seg = y[:, h * dk:(h + 1) * dk]
                segs.append(seg * (lax.rsqrt(jnp.sum(seg * seg, -1, keepdims=True) + 1e-6) * scale))
            y = jnp.concatenate(segs, axis=1)
        out_ref[...] = y.astype(out_ref.dtype)
    z = _dot(xb, wz_ref[...])
    bz_ref[...] = (z * jax.nn.sigmoid(z)).astype(bz_ref.dtype)
    small_ref[...] = _dot_narrow(x, ws_ref[...])

    c = _dot(xb, wc_ref[...])
    cos = cos_ref[...]
    sin = sin_ref[...]
    lane = lax.broadcasted_iota(jnp.int32, (tm, C_DIM), 1)
    first_half = (lane % C_HEAD_DIM) < (C_HEAD_DIM // 2)
    half = C_HEAD_DIM // 2

    def rope(t):
        swapped = jnp.where(first_half, pltpu.roll(t, C_DIM - half, 1), pltpu.roll(t, half, 1))
        return t * cos + swapped * sin

    cq_ref[...] = (rope(c[:, :C_DIM]) * (C_HEAD_DIM ** -0.5)).astype(BF16)
    ck_ref[...] = rope(c[:, C_DIM:2 * C_DIM]).astype(BF16)
    cv_ref[...] = c[:, 2 * C_DIM:].astype(BF16)


def _inproj(x2d, wa, wb, wz, wc, ws, conv_w, cos_t, sin_t, seq):
    T = x2d.shape[0]
    tm = ROW_TILE
    spt = seq // tm
    row = lambda n: pl.BlockSpec((tm, n), lambda i: (i, 0))
    prev = pl.BlockSpec((8, D_MODEL), lambda i: (jnp.maximum(i * (tm // 8) - 1, 0), 0))
    pos = pl.BlockSpec((tm, C_DIM), lambda i: (i % spt, 0))
    outs = [(A_DIM, F32), (A_DIM, F32), (B_DIM, BF16), (B_DIM, BF16), (B_DIM, BF16), (B_DIM, BF16),
            (LANES, F32), (C_DIM, BF16), (C_DIM, BF16), (C_DIM, BF16)]
    return pl.pallas_call(
        functools.partial(_inproj_kernel, spt=spt),
        grid=(T // tm,),
        in_specs=[row(D_MODEL), prev, _const_spec(wa.shape), _const_spec(wb.shape), _const_spec(wz.shape),
                  _const_spec(wc.shape), _const_spec(ws.shape), _const_spec(conv_w.shape), pos, pos],
        out_specs=[row(n) for n, _ in outs],
        out_shape=[jax.ShapeDtypeStruct((T, n), dt) for n, dt in outs],
        compiler_params=_params("parallel"),
        name="inproj",
    )(x2d, x2d, wa, wb, wz, wc, ws, conv_w, cos_t, sin_t)


def _gmlp_kernel(u_ref, v_ref, g_ref, b_ref, ws_ref, bias_ref, o_ref):
    n = u_ref.shape[0] // A_CHUNK
    r = lax.broadcasted_iota(jnp.int32, (A_CHUNK, A_CHUNK), 0)
    c = lax.broadcasted_iota(jnp.int32, (A_CHUNK, A_CHUNK), 1)
    causal = r >= c
    group = lax.broadcasted_iota(jnp.int32, (A_CHUNK, A_DIM), 1) // (A_DIM // A_GROUPS)
    ws = [jnp.where(causal, ws_ref[g], 0.0).astype(BF16) for g in range(A_GROUPS)]
    bias = bias_ref[...]
    for i in range(n):
        rows = pl.ds(i * A_CHUNK, A_CHUNK)
        vn = _layer_norm(v_ref[rows, :], g_ref[...], b_ref[...]).astype(BF16)
        mixed = bias
        for g in range(A_GROUPS):
            mixed = mixed + jnp.where(group == g, _dot(ws[g], vn), 0.0)
        o_ref[rows, :] = (u_ref[rows, :] * mixed).astype(o_ref.dtype)


def _gmlp(au, av, ln_g, ln_b, ws, bias2d):
    T = au.shape[0]
    tm = ROW_TILE
    row = pl.BlockSpec((tm, A_DIM), lambda i: (i, 0))
    return pl.pallas_call(
        _gmlp_kernel,
        grid=(T // tm,),
        in_specs=[row, row, _const_spec(ln_g.shape), _const_spec(ln_b.shape),
                  _const_spec(ws.shape), _const_spec(bias2d.shape)],
        out_specs=row,
        out_shape=jax.ShapeDtypeStruct((T, A_DIM), BF16),
        compiler_params=_params("parallel"),
        name="gmlp",
    )(au, av, ln_g, ln_b, ws, bias2d)


def _unit_lower_inverse_minus_eye(ms, blk):
    def bf(t):
        return t.astype(BF16)

    xs = [-jnp.where(blk(16), m, 0.0) for m in ms]
    ps = xs
    ys = [_dot(bf(x), bf(x)) for x in xs]
    for step in range(3):
        ps = [p + y + _dot(bf(p), bf(y)) for p, y in zip(ps, ys)]
        if step < 2:
            ys = [_dot(bf(y), bf(y)) for y in ys]
    for size in (32, 64):
        off = blk(size) & jnp.logical_not(blk(size // 2))
        ls = [jnp.where(off, m, 0.0) for m in ms]
        qs = [l + _dot(bf(p), bf(l)) for p, l in zip(ps, ls)]
        ps = [p - (q + _dot(bf(q), bf(p))) for p, q in zip(ps, qs)]
    return ps


def _dn_prep_kernel(q_ref, k_ref, v_ref, s_ref, alog_ref, dtb_ref,
                    u_ref, w_ref, qd_ref, kt_ref, a_ref, gl_ref, *, heads):
    tm = q_ref.shape[0]
    dk = B_HEAD_DIM
    C = B_CHUNK
    hp = pl.program_id(1)

    r = lax.broadcasted_iota(jnp.int32, (DN_BLOCK, DN_BLOCK), 0)
    c = lax.broadcasted_iota(jnp.int32, (DN_BLOCK, DN_BLOCK), 1)

    def blk(size):
        return (r // size) == (c // size)

    lower = (r >= c) & blk(C)
    strict = (r > c) & blk(C)
    cum_mask = jnp.where(lower, 1.0, 0.0).astype(BF16)
    first_chunk = lax.broadcasted_iota(jnp.int32, (DN_BLOCK, dk), 0) < C
    lane = lax.broadcasted_iota(jnp.int32, (DN_BLOCK, LANES), 1)

    chains = [(b, h) for b in range(tm // DN_BLOCK) for h in range(heads)]
    rows = {b: slice(b * DN_BLOCK, (b + 1) * DN_BLOCK) for b, _ in chains}
    cols = {h: slice(h * dk, (h + 1) * dk) for _, h in chains}

    gates = {}
    for b in rows:
        small = s_ref[rows[b], :]
        gates[b] = (jax.nn.sigmoid(small), -jnp.exp(alog_ref[...]) * jax.nn.softplus(small + dtb_ref[...]))
    beta, gcol = [], []
    for b, h in chains:
        sig, g_all = gates[b]
        hh = hp * heads + h
        beta.append(jnp.sum(jnp.where(lane == hh, sig, 0.0), -1, keepdims=True))
        g = jnp.sum(jnp.where(lane == B_HEADS + hh, g_all, 0.0), -1, keepdims=True)
        g1, g2, g3 = _split3(jnp.broadcast_to(g, (DN_BLOCK, dk)))
        gcol.append(_dot(cum_mask, g1) + _dot(cum_mask, g2) + _dot(cum_mask, g3))

    q = [q_ref[rows[b], cols[h]].astype(F32) for b, h in chains]
    k = [k_ref[rows[b], cols[h]].astype(F32) for b, h in chains]
    kb = [ki * bi for ki, bi in zip(k, beta)]
    kk = [_dot_nt(kbi.astype(BF16), ki.astype(BF16)) for kbi, ki in zip(kb, k)]
    qk = [_dot_nt(qi.astype(BF16), ki.astype(BF16)) for qi, ki in zip(q, k)]
    decay = [jnp.exp(jnp.where(lower, gc - gc.T, -jnp.inf)) for gc in gcol]
    m = [jnp.where(strict, kki * d, 0.0) for kki, d in zip(kk, decay)]
    t_off = _unit_lower_inverse_minus_eye(m, blk)
    e_gc = [jnp.exp(gc) for gc in gcol]
    rhs = [jnp.concatenate([v_ref[rows[b], cols[h]].astype(F32) * bi, kbi * e], axis=1)
           for (b, h), bi, kbi, e in zip(chains, beta, kb, e_gc)]
    sol = [ri + _dot(t.astype(BF16), ri.astype(BF16)) for ri, t in zip(rhs, t_off)]

    for i, (b, h) in enumerate(chains):
        u_ref[rows[b], cols[h]] = sol[i][:, :dk]
        w_ref[rows[b], cols[h]] = sol[i][:, dk:].astype(BF16)
        qd_ref[rows[b], cols[h]] = (q[i] * e_gc[i]).astype(BF16)
        glast = jnp.where(first_chunk, gcol[i][C - 1:C, :], gcol[i][2 * C - 1:2 * C, :])
        kt_ref[b, h] = (k[i] * jnp.exp(glast - gcol[i])).T.astype(BF16)
        attn = jnp.where(lower, qk[i] * decay[i], 0.0)
        a_ref[rows[b], h * C:(h + 1) * C] = (attn[:, :C] + attn[:, C:]).astype(BF16)
        gl = jnp.exp(glast)
        gl_ref[16 * b:16 * b + 8, cols[h]] = gl[0:8, :]
        gl_ref[16 * b + 8:16 * b + 16, cols[h]] = gl[C:C + 8, :]


def _dn_prep(bq, bk, bv, small, alog_row, dtb_row):
    T = bq.shape[0]
    heads = 2
    tm = ROW_TILE
    wcols = heads * B_HEAD_DIM
    nb = tm // DN_BLOCK
    qkv = pl.BlockSpec((tm, wcols), lambda i, h: (i, h))
    return pl.pallas_call(
        functools.partial(_dn_prep_kernel, heads=heads),
        grid=(T // tm, B_HEADS // heads),
        in_specs=[qkv, qkv, qkv, pl.BlockSpec((tm, LANES), lambda i, h: (i, 0)),
                  _const_spec(alog_row.shape), _const_spec(dtb_row.shape)],
        out_specs=[qkv, qkv, qkv,
                   pl.BlockSpec((nb, heads, B_HEAD_DIM, DN_BLOCK), lambda i, h: (i, h, 0, 0)),
                   pl.BlockSpec((tm, heads * B_CHUNK), lambda i, h: (i, h)),
                   pl.BlockSpec((8 * tm // B_CHUNK, wcols), lambda i, h: (i, h))],
        out_shape=[jax.ShapeDtypeStruct((T, B_DIM), F32),
                   jax.ShapeDtypeStruct((T, B_DIM), BF16),
                   jax.ShapeDtypeStruct((T, B_DIM), BF16),
                   jax.ShapeDtypeStruct((T // DN_BLOCK, B_HEADS, B_HEAD_DIM, DN_BLOCK), BF16),
                   jax.ShapeDtypeStruct((T, B_HEADS * B_CHUNK), BF16),
                   jax.ShapeDtypeStruct((8 * T // B_CHUNK, B_DIM), F32)],
        compiler_params=_params("parallel", "parallel"),
        name="dn_prep",
    )(bq, bk, bv, small, alog_row, dtb_row)


def _dn_scan_kernel(u_ref, w_ref, qd_ref, kt_ref, a_ref, gl_ref, z_ref, ng_ref, o_ref, state_ref):
    S = u_ref.shape[0]
    dk = B_HEAD_DIM
    C = B_CHUNK
    state_ref[...] = jnp.zeros_like(state_ref)

    def body(n, carry):
        for ci in range(2):
            r0 = pl.multiple_of(n * DN_BLOCK + ci * C, C)
            rows = pl.ds(r0, C)
            heads = range(B_HEADS)
            cols = [slice(h * dk, (h + 1) * dk) for h in heads]
            states = [state_ref[h] for h in heads]
            r1 = [_dot(jnp.concatenate([w_ref[rows, cols[h]], qd_ref[rows, cols[h]]], axis=0),
                       states[h].astype(BF16)) for h in heads]
            v_new = [(u_ref[rows, cols[h]] - r1[h][:C]).astype(BF16) for h in heads]
            r2 = [_dot(jnp.concatenate([a_ref[rows, h * C:(h + 1) * C],
                                        kt_ref[n, h][:, ci * C:(ci + 1) * C]], axis=0), v_new[h]) for h in heads]
            gl = gl_ref[pl.ds(pl.multiple_of((2 * n + ci) * 8, 8), 8), :]
            for h in heads:
                state_ref[h] = states[h] * gl[0:1, cols[h]] + r2[h][C:]
                o = r1[h][C:] + r2[h][:C]
                o = o * lax.rsqrt(jnp.mean(o * o, -1, keepdims=True) + 1e-6) * ng_ref[...]
                o_ref[rows, cols[h]] = (o * z_ref[rows, cols[h]].astype(F32)).astype(o_ref.dtype)
        return carry

    lax.fori_loop(0, S // DN_BLOCK, body, 0)


def _dn_scan(u, w, qd, kt, acomp, gl, zs, norm_g, batch, seq):
    wide = pl.BlockSpec((seq, B_DIM), lambda b: (b, 0))
    nblk = seq // DN_BLOCK
    return pl.pallas_call(
        _dn_scan_kernel,
        grid=(batch,),
        in_specs=[wide, wide, wide,
                  pl.BlockSpec((nblk, B_HEADS, B_HEAD_DIM, DN_BLOCK), lambda b: (b, 0, 0, 0)),
                  pl.BlockSpec((seq, B_HEADS * B_CHUNK), lambda b: (b, 0)),
                  pl.BlockSpec((8 * seq // B_CHUNK, B_DIM), lambda b: (b, 0)),
                  wide, _const_spec(norm_g.shape)],
        out_specs=wide,
        out_shape=jax.ShapeDtypeStruct((batch * seq, B_DIM), BF16),
        scratch_shapes=[pltpu.VMEM((B_HEADS, B_HEAD_DIM, B_HEAD_DIM), F32)],
        compiler_params=_params("parallel"),
        name="dn_scan",
    )(u, w, qd, kt, acomp, gl, zs, norm_g)


def _attn_kernel(q_ref, k_ref, v_ref, bias_ref, o_ref):
    S = q_ref.shape[0]
    width = q_ref.shape[1]
    heads = width // C_HEAD_DIM
    head_of_lane = lax.broadcasted_iota(jnp.int32, (ATT_BLOCK, width), 1) // C_HEAD_DIM

    def key_chunks(i):
        nk = (i + 1) * ATT_BLOCK
        return [(k0, min(ATT_KEYS, nk - k0)) for k0 in range(0, nk, ATT_KEYS)]

    def scores(i, h):
        nk = (i + 1) * ATT_BLOCK
        q = q_ref[pl.ds(i * ATT_BLOCK, ATT_BLOCK), :]
        qh = jnp.where(head_of_lane == h, q, jnp.zeros_like(q))
        return [_dot_nt(qh, k_ref[pl.ds(k0, kw), :]) + bias_ref[:, pl.ds(S - nk + k0, kw)]
                for k0, kw in key_chunks(i)]

    def softmax_pv(i, s):
        mx = functools.reduce(jnp.maximum, [jnp.max(t, axis=-1, keepdims=True) for t in s])
        e = [jnp.exp(t - mx) for t in s]
        den = sum(jnp.sum(t, axis=-1, keepdims=True) for t in e)
        acc = sum(_dot(t.astype(BF16), v_ref[pl.ds(k0, kw), :]) for t, (k0, kw) in zip(e, key_chunks(i)))
        return acc / den

    tasks = [(i, h) for i in range(S // ATT_BLOCK) for h in range(heads)]
    s_next = scores(*tasks[0])
    out = None
    for t, (i, h) in enumerate(tasks):
        s_cur = s_next
        if t + 1 < len(tasks):
            s_next = scores(*tasks[t + 1])
        pv = softmax_pv(i, s_cur)
        out = pv if h == 0 else jnp.where(head_of_lane == h, pv, out)
        if h == heads - 1:
            o_ref[pl.ds(i * ATT_BLOCK, ATT_BLOCK), :] = out.astype(o_ref.dtype)


def _attn_bias(seq):
    r = np.arange(ATT_BLOCK)[:, None]
    c = np.arange(seq)[None, :]
    dist = (seq - ATT_BLOCK) + r - c
    mult = np.zeros(dist.shape, np.float64)
    for window, dil in C_CONFIGS:
        mult += (dist >= 0) & (dist <= window) & (dist % dil == 0)
    with np.errstate(divide="ignore"):
        return jnp.asarray(np.log(mult), F32)


def _attention(cq, ck, cv, bias, batch, seq):
    width = 2 * C_HEAD_DIM
    spec = pl.BlockSpec((seq, width), lambda b, p: (b, p))
    return pl.pallas_call(
        _attn_kernel,
        grid=(batch, C_DIM // width),
        in_specs=[spec, spec, spec, _const_spec(bias.shape)],
        out_specs=spec,
        out_shape=jax.ShapeDtypeStruct((batch * seq, C_DIM), BF16),
        compiler_params=_params("parallel", "parallel"),
        name="dilated_attn",
    )(cq, ck, cv, bias)


def _outproj_kernel(x_ref, ya_ref, yb_ref, yc_ref, wa_ref, wb_ref, wc_ref, g_ref, b_ref, o_ref):
    mix = _dot(ya_ref[...], wa_ref[...]) + _dot(yb_ref[...], wb_ref[...]) + _dot(yc_ref[...], wc_ref[...])
    o_ref[...] = _layer_norm(DN_ALPHA * x_ref[...] + mix, g_ref[...], b_ref[...])


def _outproj(x2d, ya, yb, yc, wa, wb, wc, g, b):
    T = x2d.shape[0]
    tm = ROW_TILE
    row = lambda n: pl.BlockSpec((tm, n), lambda i: (i, 0))
    return pl.pallas_call(
        _outproj_kernel,
        grid=(T // tm,),
        in_specs=[row(D_MODEL), row(A_DIM), row(B_DIM), row(C_DIM), _const_spec(wa.shape),
                  _const_spec(wb.shape), _const_spec(wc.shape), _const_spec(g.shape), _const_spec(b.shape)],
        out_specs=row(D_MODEL),
        out_shape=jax.ShapeDtypeStruct((T, D_MODEL), F32),
        compiler_params=_params("parallel"),
        name="outproj_ln",
    )(x2d, ya, yb, yc, wa, wb, wc, g, b)


def _swiglu_partial(xb, wg, wu, wd):
    gate = _dot(xb, wg)
    up = _dot(xb, wu)
    h = (gate * jax.nn.sigmoid(gate) * up).astype(BF16)
    return _dot(h, wd)


def _ffn_kernel(x_ref, wg_ref, wu_ref, wd_ref, g_ref, b_ref, o_ref, acc_ref):
    j = pl.program_id(1)
    part = _swiglu_partial(x_ref[...].astype(BF16), wg_ref[0], wu_ref[0], wd_ref[0])

    @pl.when(j == 0)
    def _():
        acc_ref[...] = part

    @pl.when(j > 0)
    def _():
        acc_ref[...] += part

    @pl.when(j == pl.num_programs(1) - 1)
    def _():
        o_ref[...] = _layer_norm(DN_ALPHA * x_ref[...] + acc_ref[...], g_ref[...], b_ref[...])


def _ffn(x2d, wg, wu, wd, li, g, b):
    T = x2d.shape[0]
    tm = ROW_TILE
    tf = FF_CHUNK
    return pl.pallas_call(
        _ffn_kernel,
        grid=(T // tm, D_FF // tf),
        in_specs=[pl.BlockSpec((tm, D_MODEL), lambda i, j: (i, 0)),
                  pl.BlockSpec((1, D_MODEL, tf), lambda i, j: (li, 0, j)),
                  pl.BlockSpec((1, D_MODEL, tf), lambda i, j: (li, 0, j)),
                  pl.BlockSpec((1, tf, D_MODEL), lambda i, j: (li, j, 0)),
                  _const_spec(g.shape), _const_spec(b.shape)],
        out_specs=pl.BlockSpec((tm, D_MODEL), lambda i, j: (i, 0)),
        out_shape=jax.ShapeDtypeStruct((T, D_MODEL), F32),
        scratch_shapes=[pltpu.VMEM((tm, D_MODEL), F32)],
        compiler_params=_params("parallel", "arbitrary"),
        name="ffn_ln",
    )(x2d, wg, wu, wd, g, b)


def _router_kernel(x_ref, w_ref, route_ref, counts_ref, carry_ref):
    i = pl.program_id(0)
    tm = x_ref.shape[0]

    @pl.when(i == 0)
    def _():
        carry_ref[...] = jnp.zeros_like(carry_ref)

    logits = _dot_narrow(x_ref[...], w_ref[...])
    lane = lax.broadcasted_iota(jnp.int32, (tm, LANES), 1)
    logits = jnp.where(lane < N_EXPERTS, logits, -jnp.inf)
    m1 = jnp.max(logits, axis=-1, keepdims=True)
    e1 = jnp.min(jnp.where(logits == m1, lane, LANES), axis=-1, keepdims=True)
    rest = jnp.where(lane == e1, -jnp.inf, logits)
    m2 = jnp.max(rest, axis=-1, keepdims=True)
    e2 = jnp.min(jnp.where(rest == m2, lane, LANES), axis=-1, keepdims=True)
    t = jnp.exp(m2 - m1)
    p1 = 1.0 / (1.0 + t)
    p2 = t / (1.0 + t)
    hot1 = lane == e1
    hot2 = lane == e2
    hot = jnp.where(hot1 | hot2, 1.0, 0.0)
    rr = lax.broadcasted_iota(jnp.int32, (tm, tm), 0)
    cc = lax.broadcasted_iota(jnp.int32, (tm, tm), 1)
    before = jnp.where(rr > cc, 1.0, 0.0).astype(BF16)
    cnt = _dot(before, hot.astype(BF16)) + carry_ref[0:1, :]
    rank1 = jnp.sum(jnp.where(hot1, cnt, 0.0), axis=-1, keepdims=True)
    rank2 = jnp.sum(jnp.where(hot2, cnt, 0.0), axis=-1, keepdims=True)
    route = jnp.where(lane == 0, e1.astype(F32), 0.0)
    route = jnp.where(lane == 1, e2.astype(F32), route)
    route = jnp.where(lane == 2, p1, route)
    route = jnp.where(lane == 3, p2, route)
    route = jnp.where(lane == 4, rank1, route)
    route = jnp.where(lane == 5, rank2, route)
    route_ref[...] = route
    carry_ref[...] = carry_ref[...] + jnp.sum(hot, axis=0, keepdims=True)
    counts_ref[...] = carry_ref[...]


def _router(x2d, w_pad):
    T = x2d.shape[0]
    tm = ROW_TILE
    return pl.pallas_call(
        _router_kernel,
        grid=(T // tm,),
        in_specs=[pl.BlockSpec((tm, D_MODEL), lambda i: (i, 0)), _const_spec(w_pad.shape)],
        out_specs=[pl.BlockSpec((tm, LANES), lambda i: (i, 0)), _const_spec((8, LANES))],
        out_shape=[jax.ShapeDtypeStruct((T, LANES), F32), jax.ShapeDtypeStruct((8, LANES), F32)],
        scratch_shapes=[pltpu.VMEM((8, LANES), F32)],
        compiler_params=_params("arbitrary"),
        name="moe_router",
    )(x2d, w_pad)


def _start_row_copies(n, make_copies):
    def step(g, c):
        for j in range(DMA_UNROLL):
            for idx, cp in enumerate(make_copies(g * DMA_UNROLL + j)):
                cp.start(priority=(j + idx) % 2)
        return c

    lax.fori_loop(0, n // DMA_UNROLL, step, 0)


def _dispatch_kernel(pos0_ref, pos1_ref, x_ref, init_ref, xs_ref, sem):
    del init_ref
    n = x_ref.shape[0]

    def copies(t):
        src = x_ref.at[pl.ds(t, 1), :]
        return (pltpu.make_async_copy(src, xs_ref.at[pl.ds(pos0_ref[0, 0, t], 1), :], sem),
                pltpu.make_async_copy(src, xs_ref.at[pl.ds(pos1_ref[0, 0, t], 1), :], sem))

    _start_row_copies(n, copies)
    for _ in range(2):
        pltpu.make_async_copy(x_ref, xs_ref.at[pl.ds(0, n), :], sem).wait()


def _dispatch(x2d, pos0, pos1, n_rows):
    T = x2d.shape[0]
    td = DMA_ROWS
    idx = pl.BlockSpec((1, 1, td), lambda i: (i, 0, 0), memory_space=pltpu.SMEM)
    init = jnp.zeros((n_rows, D_MODEL), F32)
    return pl.pallas_call(
        _dispatch_kernel,
        grid=(T // td,),
        in_specs=[idx, idx, pl.BlockSpec((td, D_MODEL), lambda i: (i, 0)),
                  pl.BlockSpec(memory_space=pl.ANY)],
        out_specs=pl.BlockSpec(memory_space=pl.ANY),
        out_shape=jax.ShapeDtypeStruct((n_rows, D_MODEL), F32),
        scratch_shapes=[pltpu.SemaphoreType.DMA(())],
        input_output_aliases={3: 0},
        compiler_params=_params("arbitrary"),
        name="moe_dispatch",
    )(pos0.reshape(T // td, 1, td), pos1.reshape(T // td, 1, td), x2d, init)


def _combine_kernel(pos0_ref, pos1_ref, x_ref, route_ref, ys_ref, g_ref, b_ref, o_ref, buf0, buf1, sem):
    n = x_ref.shape[0]

    def copies(t):
        return (pltpu.make_async_copy(ys_ref.at[pl.ds(pos0_ref[0, 0, t], 1), :], buf0.at[pl.ds(t, 1), :], sem),
                pltpu.make_async_copy(ys_ref.at[pl.ds(pos1_ref[0, 0, t], 1), :], buf1.at[pl.ds(t, 1), :], sem))

    _start_row_copies(n, copies)
    for buf in (buf0, buf1):
        pltpu.make_async_copy(ys_ref.at[pl.ds(0, n), :], buf, sem).wait()
    route = route_ref[...]
    f = route[:, 2:3] * buf0[...] + route[:, 3:4] * buf1[...]
    o_ref[...] = _layer_norm(DN_ALPHA * x_ref[...] + f, g_ref[...], b_ref[...])


def _combine(x2d, route, ys, pos0, pos1, g, b):
    T = x2d.shape[0]
    td = DMA_ROWS
    idx = pl.BlockSpec((1, 1, td), lambda i: (i, 0, 0), memory_space=pltpu.SMEM)
    return pl.pallas_call(
        _combine_kernel,
        grid=(T // td,),
        in_specs=[idx, idx, pl.BlockSpec((td, D_MODEL), lambda i: (i, 0)),
                  pl.BlockSpec((td, LANES), lambda i: (i, 0)),
                  pl.BlockSpec(memory_space=pl.ANY), _const_spec(g.shape), _const_spec(b.shape)],
        out_specs=pl.BlockSpec((td, D_MODEL), lambda i: (i, 0)),
        out_shape=jax.ShapeDtypeStruct((T, D_MODEL), F32),
        scratch_shapes=[pltpu.VMEM((td, D_MODEL), F32), pltpu.VMEM((td, D_MODEL), F32),
                        pltpu.SemaphoreType.DMA(())],
        compiler_params=_params("arbitrary"),
        name="moe_combine_ln",
    )(pos0.reshape(T // td, 1, td), pos1.reshape(T // td, 1, td), x2d, route, ys, g, b)


def _moe_ffn_kernel(te_ref, nt_ref, x_ref, wg_ref, wu_ref, wd_ref, o_ref, acc_ref):
    i = pl.program_id(0)
    j = pl.program_id(1)

    @pl.when(i < nt_ref[0])
    def _():
        part = _swiglu_partial(x_ref[...].astype(BF16), wg_ref[0], wu_ref[0], wd_ref[0])

        @pl.when(j == 0)
        def _():
            acc_ref[...] = part

        @pl.when(j > 0)
        def _():
            acc_ref[...] += part

        @pl.when(j == pl.num_programs(1) - 1)
        def _():
            o_ref[...] = acc_ref[...]

    @pl.when((i >= nt_ref[0]) & (j == pl.num_programs(1) - 1))
    def _():
        o_ref[...] = jnp.zeros_like(o_ref)


def _moe_ffn(xs, tile_expert, n_tiles, wg, wu, wd):
    n_rows = xs.shape[0]
    tm = MOE_TILE
    tf = FF_CHUNK
    nf = D_FF // tf

    def row_map(i, j, te, nt):
        return (jnp.minimum(i, nt[0] - 1), 0)

    def ee(i, te, nt):
        return te[jnp.minimum(i, nt[0] - 1)]

    def jj(i, j, nt):
        return jnp.where(i < nt[0], j, nf - 1)

    grid_spec = pltpu.PrefetchScalarGridSpec(
        num_scalar_prefetch=2,
        grid=(n_rows // tm, nf),
        in_specs=[pl.BlockSpec((tm, D_MODEL), row_map),
                  pl.BlockSpec((1, D_MODEL, tf), lambda i, j, te, nt: (ee(i, te, nt), 0, jj(i, j, nt))),
                  pl.BlockSpec((1, D_MODEL, tf), lambda i, j, te, nt: (ee(i, te, nt), 0, jj(i, j, nt))),
                  pl.BlockSpec((1, tf, D_MODEL), lambda i, j, te, nt: (ee(i, te, nt), jj(i, j, nt), 0))],
        out_specs=pl.BlockSpec((tm, D_MODEL), lambda i, j, te, nt: (i, 0)),
        scratch_shapes=[pltpu.VMEM((tm, D_MODEL), F32)],
    )
    return pl.pallas_call(
        _moe_ffn_kernel,
        grid_spec=grid_spec,
        out_shape=jax.ShapeDtypeStruct((n_rows, D_MODEL), F32),
        compiler_params=_params("arbitrary", "arbitrary"),
        name="moe_ffn",
    )(tile_expert, n_tiles, xs, wg, wu, wd)


def _moe(x2d, w_router, wg, wu, wd, li, g, b):
    T = x2d.shape[0]
    tm = MOE_TILE
    n_tiles_max = (2 * T) // tm + N_EXPERTS
    n_rows = n_tiles_max * tm
    route, counts = _router(x2d, _pack_narrow(w_router))
    cnt = counts[0, :N_EXPERTS].astype(jnp.int32)
    padded = ((cnt + tm - 1) // tm) * tm
    ends = jnp.cumsum(padded)
    offs = ends - padded
    e = route[:, 0:2].astype(jnp.int32)
    rank = route[:, 4:6].astype(jnp.int32)
    pos = offs[e] + rank
    pos0, pos1 = pos[:, 0], pos[:, 1]
    tile_start = jnp.arange(n_tiles_max, dtype=jnp.int32) * tm
    tile_expert = jnp.minimum(jnp.sum(tile_start[:, None] >= ends[None, :], axis=1), N_EXPERTS - 1).astype(jnp.int32)
    n_tiles = (ends[-1:] // tm).astype(jnp.int32)
    xs = _dispatch(x2d, pos0, pos1, n_rows)
    ys = _moe_ffn(xs, tile_expert + li * N_EXPERTS, n_tiles, wg, wu, wd)
    return _combine(x2d, route, ys, pos0, pos1, g, b)


def _rope_tables(seq):
    inv_freq = ROPE_THETA ** (-jnp.arange(0, C_HEAD_DIM, 2, dtype=F32) / C_HEAD_DIM)
    ang = jnp.arange(seq, dtype=F32)[:, None] * inv_freq[None, :]
    cos, sin = jnp.cos(ang), jnp.sin(ang)
    cos_h = jnp.concatenate([cos, cos], axis=-1)
    sin_h = jnp.concatenate([-sin, sin], axis=-1)
    return jnp.tile(cos_h, (1, C_HEADS)), jnp.tile(sin_h, (1, C_HEADS))


def kernel(x, w_in, conv_w, a_ln_g, a_ln_b, a_ws, a_bs, b_a_log, b_dt_bias, b_norm_g, w_out, ln1_g, ln1_b, ln2_g, ln2_b, ffn_w_gate, ffn_w_up, ffn_w_down, moe_router, moe_w_gate, moe_w_up, moe_w_down):
    batch, seq, _ = x.shape
    T = batch * seq
    cos_t, sin_t = _rope_tables(seq)
    att_bias = _attn_bias(seq)
    o_a, o_bq, o_bz, o_beta, o_cq = 0, 2 * A_DIM, 2 * A_DIM + 3 * B_DIM, 2 * A_DIM + 4 * B_DIM, 2 * A_DIM + 4 * B_DIM + 2 * B_HEADS
    row = lambda v: v.reshape(1, -1)
    h2d = x.reshape(T, D_MODEL)
    dense_w = [t.astype(BF16) for t in (ffn_w_gate, ffn_w_up, ffn_w_down)]
    moe_w = [t.astype(BF16).reshape((-1,) + t.shape[2:]) for t in (moe_w_gate, moe_w_up, moe_w_down)]
    for layer in range(DEPTH):
        w = w_in[layer]
        wa = w[:, o_a:o_bq].astype(BF16)
        wb = w[:, o_bq:o_bz].astype(BF16)
        wz = w[:, o_bz:o_beta].astype(BF16)
        ws = _pack_narrow(w[:, o_beta:o_cq])
        wc = w[:, o_cq:].astype(BF16)
        au, av, bq, bk, bv, zs, small, cq, ck, cv = _inproj(h2d, wa, wb, wz, wc, ws, conv_w[layer], cos_t, sin_t, seq)

        bias2d = jnp.repeat(a_bs[layer].T, A_DIM // A_GROUPS, axis=1)
        ya = _gmlp(au, av, row(a_ln_g[layer]), row(a_ln_b[layer]), a_ws[layer], bias2d)

        zeros_row = jnp.zeros((1, LANES), F32)
        alog_row = zeros_row.at[0, B_HEADS:2 * B_HEADS].set(b_a_log[layer])
        dtb_row = zeros_row.at[0, B_HEADS:2 * B_HEADS].set(b_dt_bias[layer])
        u, wv, qd, kt, acomp, gl = _dn_prep(bq, bk, bv, small, alog_row, dtb_row)
        yb = _dn_scan(u, wv, qd, kt, acomp, gl, zs, row(b_norm_g[layer]), batch, seq)

        yc = _attention(cq, ck, cv, att_bias, batch, seq)

        wo = w_out[layer].astype(BF16)
        h2d = _outproj(h2d, ya, yb, yc, wo[:A_DIM], wo[A_DIM:A_DIM + B_DIM], wo[A_DIM + B_DIM:],
                       row(ln1_g[layer]), row(ln1_b[layer]))
        i = layer // 2
        if layer % 2 == 0:
            h2d = _ffn(h2d, *dense_w, i, row(ln2_g[layer]), row(ln2_b[layer]))
        else:
            h2d = _moe(h2d, moe_router[i], *moe_w, i, row(ln2_g[layer]), row(ln2_b[layer]))
    return h2d.reshape(batch, seq, D_MODEL)
```

```python
import functools

import jax
import jax.numpy as jnp
import numpy as np
from jax import lax
from jax.experimental import pallas as pl
from jax.experimental.pallas import tpu as pltpu

F32 = jnp.float32
BF16 = jnp.bfloat16

D_MODEL = 1024
DEPTH = 4
A_DIM = 256
A_GROUPS = 4
A_CHUNK = 128
B_HEAD_DIM = 128
B_DIM = 512
B_HEADS = 4
B_CONV = 4
B_CHUNK = 64
C_HEAD_DIM = 64
C_DIM = 256
C_HEADS = 4
C_CONFIGS = ((128, 1), (512, 4), (2048, 16))
ROPE_THETA = 10000.0
D_FF = 2816
N_EXPERTS = 8
DN_ALPHA = (2.0 * DEPTH) ** 0.25
LN_EPS = 1e-5

LANES = 128
VMEM_LIMIT = 56 * 1024 * 1024
ROW_TILE = 512
ATT_BLOCK = 128
ATT_KEYS = 256
DN_BLOCK = 2 * B_CHUNK
DN_SCAN_SEQS = 2
DN_SCAN_ROWS = 1024
MOE_TILE = 512
FF_SUB = 256
DMA_ROWS = 256
DMA_UNROLL = 8


def _params(*sem):
    return pltpu.CompilerParams(dimension_semantics=sem, vmem_limit_bytes=VMEM_LIMIT)


def _dot(a, b):
    return jnp.dot(a, b, preferred_element_type=F32)


def _dot_nt(a, b):
    return lax.dot_general(a, b, (((1,), (1,)), ((), ())), preferred_element_type=F32)


def _split3(x):
    x1 = x.astype(BF16)
    r = x - x1.astype(F32)
    x2 = r.astype(BF16)
    x3 = (r - x2.astype(F32)).astype(BF16)
    return x1, x2, x3


NARROW = 8


def _pack_narrow(w):
    pieces = _split3(w)
    packed = jnp.zeros((w.shape[0], LANES), BF16)
    for i, piece in enumerate(pieces):
        packed = packed.at[:, i * NARROW:(i + 1) * NARROW].set(piece)
    return packed


def _dot_narrow(x, w_packed):
    x1 = x.astype(BF16)
    x2 = (x - x1.astype(F32)).astype(BF16)
    p = _dot(x1, w_packed) + _dot(x2, w_packed)
    p = p + pltpu.roll(p, LANES - NARROW, 1) + pltpu.roll(p, LANES - 2 * NARROW, 1)
    lane = lax.broadcasted_iota(jnp.int32, p.shape, 1)
    return jnp.where(lane < NARROW, p, 0.0)


def _gelu_tanh(x):
    z = np.sqrt(2.0 / np.pi) * (x + 0.044715 * (x * x * x))
    return x * jax.nn.sigmoid(2.0 * z)


def _layer_norm(y, g, b):
    mu = jnp.mean(y, axis=-1, keepdims=True)
    yc = y - mu
    var = jnp.mean(yc * yc, axis=-1, keepdims=True)
    return yc * lax.rsqrt(var + LN_EPS) * g + b


def _const_spec(shape):
    nd = len(shape)
    return pl.BlockSpec(shape, lambda *_: (0,) * nd)


def _inproj_kernel(x_ref, xp_ref, wa_ref, wb_ref, wz_ref, wc_ref, ws_ref, cw_ref, cos_ref, sin_ref,
                   au_ref, av_ref, bq_ref, bk_ref, bv_ref, bz_ref, small_ref, cq_ref, ck_ref, cv_ref, *, spt):
    tm = x_ref.shape[0]
    x = x_ref[...]
    xb = x.astype(BF16)
    a = _gelu_tanh(_dot(xb, wa_ref[...]))
    au_ref[...] = a[:, :A_DIM]
    av_ref[...] = a[:, A_DIM:]

    seq_start = (pl.program_id(0) % spt) == 0
    xpb = jnp.where(seq_start, 0.0, xp_ref[...]).astype(BF16)
    dk = B_HEAD_DIM
    for part, out_ref in enumerate((bq_ref, bk_ref, bv_ref)):
        cols = slice(part * B_DIM, (part + 1) * B_DIM)
        w = wb_ref[:, cols]
        cur = _dot(xb, w)
        ext = jnp.concatenate([_dot(xpb, w), cur], axis=0)
        cw = cw_ref[:, cols]
        acc = cur * cw[B_CONV - 1:B_CONV, :]
        for s in range(1, B_CONV):
            acc = acc + pltpu.roll(ext, s, 0)[8:, :] * cw[B_CONV - 1 - s:B_CONV - s, :]
        y = acc * jax.nn.sigmoid(acc)
        if part < 2:
            scale = dk ** -0.5 if part == 0 else 1.0
            segs = []
            for h in range(B_HEADS):
                seg = y[:, h * dk:(h + 1) * dk]
                segs.append(seg * (lax.rsqrt(jnp.sum(seg * seg, -1, keepdims=True) + 1e-6) * scale))
            y = jnp.concatenate(segs, axis=1)
        out_ref[...] = y.astype(out_ref.dtype)
    z = _dot(xb, wz_ref[...])
    bz_ref[...] = (z * jax.nn.sigmoid(z)).astype(bz_ref.dtype)
    small_ref[...] = _dot_narrow(x, ws_ref[...])

    c = _dot(xb, wc_ref[...])
    cos = cos_ref[...]
    sin = sin_ref[...]
    lane = lax.broadcasted_iota(jnp.int32, (tm, C_DIM), 1)
    first_half = (lane % C_HEAD_DIM) < (C_HEAD_DIM // 2)
    half = C_HEAD_DIM // 2

    def rope(t):
        swapped = jnp.where(first_half, pltpu.roll(t, C_DIM - half, 1), pltpu.roll(t, half, 1))
        return t * cos + swapped * sin

    cq_ref[...] = (rope(c[:, :C_DIM]) * (C_HEAD_DIM ** -0.5)).astype(BF16)
    ck_ref[...] = rope(c[:, C_DIM:2 * C_DIM]).astype(BF16)
    cv_ref[...] = c[:, 2 * C_DIM:].astype(BF16)


def _inproj(x2d, wa, wb, wz, wc, ws, conv_w, cos_t, sin_t, seq):
    T = x2d.shape[0]
    tm = ROW_TILE
    spt = seq // tm
    row = lambda n: pl.BlockSpec((tm, n), lambda i: (i, 0))
    prev = pl.BlockSpec((8, D_MODEL), lambda i: (jnp.maximum(i * (tm // 8) - 1, 0), 0))
    pos = pl.BlockSpec((tm, C_DIM), lambda i: (i % spt, 0))
    outs = [(A_DIM, F32), (A_DIM, F32), (B_DIM, BF16), (B_DIM, BF16), (B_DIM, BF16), (B_DIM, BF16),
            (LANES, F32), (C_DIM, BF16), (C_DIM, BF16), (C_DIM, BF16)]
    return pl.pallas_call(
        functools.partial(_inproj_kernel, spt=spt),
        grid=(T // tm,),
        in_specs=[row(D_MODEL), prev, _const_spec(wa.shape), _const_spec(wb.shape), _const_spec(wz.shape),
                  _const_spec(wc.shape), _const_spec(ws.shape), _const_spec(conv_w.shape), pos, pos],
        out_specs=[row(n) for n, _ in outs],
        out_shape=[jax.ShapeDtypeStruct((T, n), dt) for n, dt in outs],
        compiler_params=_params("parallel"),
        name="inproj",
    )(x2d, x2d, wa, wb, wz, wc, ws, conv_w, cos_t, sin_t)


def _gmlp_kernel(u_ref, v_ref, g_ref, b_ref, ws_ref, bias_ref, o_ref):
    n = u_ref.shape[0] // A_CHUNK
    r = lax.broadcasted_iota(jnp.int32, (A_CHUNK, A_CHUNK), 0)
    c = lax.broadcasted_iota(jnp.int32, (A_CHUNK, A_CHUNK), 1)
    causal = r >= c
    group = lax.broadcasted_iota(jnp.int32, (A_CHUNK, A_DIM), 1) // (A_DIM // A_GROUPS)
    ws = [jnp.where(causal, ws_ref[g], 0.0).astype(BF16) for g in range(A_GROUPS)]
    bias = bias_ref[...]
    for i in range(n):
        rows = pl.ds(i * A_CHUNK, A_CHUNK)
        vn = _layer_norm(v_ref[rows, :], g_ref[...], b_ref[...]).astype(BF16)
        mixed = bias
        for g in range(A_GROUPS):
            mixed = mixed + jnp.where(group == g, _dot(ws[g], vn), 0.0)
        o_ref[rows, :] = (u_ref[rows, :] * mixed).astype(o_ref.dtype)


def _gmlp(au, av, ln_g, ln_b, ws, bias2d):
    T = au.shape[0]
    tm = ROW_TILE
    row = pl.BlockSpec((tm, A_DIM), lambda i: (i, 0))
    return pl.pallas_call(
        _gmlp_kernel,
        grid=(T // tm,),
        in_specs=[row, row, _const_spec(ln_g.shape), _const_spec(ln_b.shape),
                  _const_spec(ws.shape), _const_spec(bias2d.shape)],
        out_specs=row,
        out_shape=jax.ShapeDtypeStruct((T, A_DIM), BF16),
        compiler_params=_params("parallel"),
        name="gmlp",
    )(au, av, ln_g, ln_b, ws, bias2d)


def _unit_lower_inverse_minus_eye(ms, blk):
    def bf(t):
        return t.astype(BF16)

    xs = [-jnp.where(blk(16), m, 0.0) for m in ms]
    ps = xs
    ys = [_dot(bf(x), bf(x)) for x in xs]
    for step in range(3):
        ps = [p + y + _dot(bf(p), bf(y)) for p, y in zip(ps, ys)]
        if step < 2:
            ys = [_dot(bf(y), bf(y)) for y in ys]
    for size in (32, 64):
        off = blk(size) & jnp.logical_not(blk(size // 2))
        ls = [jnp.where(off, m, 0.0) for m in ms]
        qs = [l + _dot(bf(p), bf(l)) for p, l in zip(ps, ls)]
        ps = [p - (q + _dot(bf(q), bf(p))) for p, q in zip(ps, qs)]
    return ps


def _dn_prep_kernel(q_ref, k_ref, v_ref, s_ref, alog_ref, dtb_ref,
                    u_ref, w_ref, qd_ref, kt_ref, a_ref, gl_ref, *, heads):
    tm = q_ref.shape[0]
    dk = B_HEAD_DIM
    C = B_CHUNK
    hp = pl.program_id(1)

    r = lax.broadcasted_iota(jnp.int32, (DN_BLOCK, DN_BLOCK), 0)
    c = lax.broadcasted_iota(jnp.int32, (DN_BLOCK, DN_BLOCK), 1)

    def blk(size):
        return (r // size) == (c // size)

    lower = (r >= c) & blk(C)
    strict = (r > c) & blk(C)
    cum_mask = jnp.where(lower, 1.0, 0.0).astype(BF16)
    first_chunk = lax.broadcasted_iota(jnp.int32, (DN_BLOCK, dk), 0) < C
    lane = lax.broadcasted_iota(jnp.int32, (DN_BLOCK, LANES), 1)

    chains = [(b, h) for b in range(tm // DN_BLOCK) for h in range(heads)]
    rows = {b: slice(b * DN_BLOCK, (b + 1) * DN_BLOCK) for b, _ in chains}
    cols = {h: slice(h * dk, (h + 1) * dk) for _, h in chains}

    gates = {}
    for b in rows:
        small = s_ref[rows[b], :]
        gates[b] = (jax.nn.sigmoid(small), -jnp.exp(alog_ref[...]) * jax.nn.softplus(small + dtb_ref[...]))
    beta, gcol = [], []
    for b, h in chains:
        sig, g_all = gates[b]
        hh = hp * heads + h
        beta.append(jnp.sum(jnp.where(lane == hh, sig, 0.0), -1, keepdims=True))
        g = jnp.sum(jnp.where(lane == B_HEADS + hh, g_all, 0.0), -1, keepdims=True)
        g1, g2, g3 = _split3(jnp.broadcast_to(g, (DN_BLOCK, dk)))
        gcol.append(_dot(cum_mask, g1) + _dot(cum_mask, g2) + _dot(cum_mask, g3))

    q = [q_ref[rows[b], cols[h]].astype(F32) for b, h in chains]
    k = [k_ref[rows[b], cols[h]].astype(F32) for b, h in chains]
    kb = [ki * bi for ki, bi in zip(k, beta)]
    kk = [_dot_nt(kbi.astype(BF16), ki.astype(BF16)) for kbi, ki in zip(kb, k)]
    qk = [_dot_nt(qi.astype(BF16), ki.astype(BF16)) for qi, ki in zip(q, k)]
    decay = [jnp.exp(jnp.where(lower, gc - gc.T, -jnp.inf)) for gc in gcol]
    m = [jnp.where(strict, kki * d, 0.0) for kki, d in zip(kk, decay)]
    t_off = _unit_lower_inverse_minus_eye(m, blk)
    e_gc = [jnp.exp(gc) for gc in gcol]
    rhs = [jnp.concatenate([v_ref[rows[b], cols[h]].astype(F32) * bi, kbi * e], axis=1)
           for (b, h), bi, kbi, e in zip(chains, beta, kb, e_gc)]
    sol = [ri + _dot(t.astype(BF16), ri.astype(BF16)) for ri, t in zip(rhs, t_off)]

    for i, (b, h) in enumerate(chains):
        u_ref[rows[b], cols[h]] = sol[i][:, :dk]
        w_ref[rows[b], cols[h]] = sol[i][:, dk:].astype(BF16)
        qd_ref[rows[b], cols[h]] = (q[i] * e_gc[i]).astype(BF16)
        glast = jnp.where(first_chunk, gcol[i][C - 1:C, :], gcol[i][2 * C - 1:2 * C, :])
        kt_ref[b, h] = (k[i] * jnp.exp(glast - gcol[i])).T.astype(BF16)
        attn = jnp.where(lower, qk[i] * decay[i], 0.0)
        a_ref[rows[b], h * C:(h + 1) * C] = (attn[:, :C] + attn[:, C:]).astype(BF16)
        gl = jnp.exp(glast)
        gl_ref[16 * b:16 * b + 8, cols[h]] = gl[0:8, :]
        gl_ref[16 * b + 8:16 * b + 16, cols[h]] = gl[C:C + 8, :]


def _dn_prep(bq, bk, bv, small, alog_row, dtb_row):
    T = bq.shape[0]
    heads = 2
    tm = ROW_TILE
    wcols = heads * B_HEAD_DIM
    nb = tm // DN_BLOCK
    qkv = pl.BlockSpec((tm, wcols), lambda i, h: (i, h))
    return pl.pallas_call(
        functools.partial(_dn_prep_kernel, heads=heads),
        grid=(T // tm, B_HEADS // heads),
        in_specs=[qkv, qkv, qkv, pl.BlockSpec((tm, LANES), lambda i, h: (i, 0)),
                  _const_spec(alog_row.shape), _const_spec(dtb_row.shape)],
        out_specs=[qkv, qkv, qkv,
                   pl.BlockSpec((nb, heads, B_HEAD_DIM, DN_BLOCK), lambda i, h: (i, h, 0, 0)),
                   pl.BlockSpec((tm, heads * B_CHUNK), lambda i, h: (i, h)),
                   pl.BlockSpec((8 * tm // B_CHUNK, wcols), lambda i, h: (i, h))],
        out_shape=[jax.ShapeDtypeStruct((T, B_DIM), F32),
                   jax.ShapeDtypeStruct((T, B_DIM), BF16),
                   jax.ShapeDtypeStruct((T, B_DIM), BF16),
                   jax.ShapeDtypeStruct((T // DN_BLOCK, B_HEADS, B_HEAD_DIM, DN_BLOCK), BF16),
                   jax.ShapeDtypeStruct((T, B_HEADS * B_CHUNK), BF16),
                   jax.ShapeDtypeStruct((8 * T // B_CHUNK, B_DIM), F32)],
        compiler_params=_params("parallel", "parallel"),
        name="dn_prep",
    )(bq, bk, bv, small, alog_row, dtb_row)


def _dn_scan_kernel(u_ref, w_ref, qd_ref, kt_ref, a_ref, gl_ref, z_ref, ng_ref, o_ref, state_ref):
    nb, rows_per_step = u_ref.shape[0], u_ref.shape[1]
    dk = B_HEAD_DIM
    C = B_CHUNK

    @pl.when(pl.program_id(1) == 0)
    def _():
        state_ref[...] = jnp.zeros_like(state_ref)

    chains = [(s, h) for s in range(nb) for h in range(B_HEADS)]
    cols = [slice(h * dk, (h + 1) * dk) for h in range(B_HEADS)]

    def body(n, carry):
        for ci in range(2):
            rows = pl.ds(pl.multiple_of(n * DN_BLOCK + ci * C, C), C)
            glr = pl.ds(pl.multiple_of((2 * n + ci) * 8, 8), 8)
            states = [state_ref[s, h] for s, h in chains]
            r1 = [_dot(jnp.concatenate([w_ref[s, rows, cols[h]], qd_ref[s, rows, cols[h]]], axis=0),
                       st.astype(BF16)) for (s, h), st in zip(chains, states)]
            v_new = [(u_ref[s, rows, cols[h]] - r[:C]).astype(BF16) for (s, h), r in zip(chains, r1)]
            r2 = [_dot(jnp.concatenate([a_ref[s, rows, h * C:(h + 1) * C],
                                        kt_ref[s, n, h][:, ci * C:(ci + 1) * C]], axis=0), v)
                  for (s, h), v in zip(chains, v_new)]
            for i, (s, h) in enumerate(chains):
                state_ref[s, h] = states[i] * gl_ref[s, glr, cols[h]][0:1, :] + r2[i][C:]
                o = r1[i][C:] + r2[i][:C]
                o = o * lax.rsqrt(jnp.mean(o * o, -1, keepdims=True) + 1e-6) * ng_ref[...]
                o_ref[s, rows, cols[h]] = (o * z_ref[s, rows, cols[h]].astype(F32)).astype(o_ref.dtype)
        return carry

    lax.fori_loop(0, rows_per_step // DN_BLOCK, body, 0)


def _dn_scan(u, w, qd, kt, acomp, gl, zs, norm_g, batch, seq):
    nb = DN_SCAN_SEQS
    rows = DN_SCAN_ROWS

    def per_seq(arr, rows_of_block):
        arr = arr.reshape((batch, arr.shape[0] // batch) + arr.shape[1:])
        block = (nb, rows_of_block) + arr.shape[2:]
        return arr, pl.BlockSpec(block, lambda b, j: (b, j) + (0,) * (len(block) - 2))

    ins = [per_seq(u, rows), per_seq(w, rows), per_seq(qd, rows), per_seq(kt, rows // DN_BLOCK),
           per_seq(acomp, rows), per_seq(gl, 8 * rows // B_CHUNK), per_seq(zs, rows)]
    out = pl.pallas_call(
        _dn_scan_kernel,
        grid=(batch // nb, seq // rows),
        in_specs=[spec for _, spec in ins] + [_const_spec(norm_g.shape)],
        out_specs=pl.BlockSpec((nb, rows, B_DIM), lambda b, j: (b, j, 0)),
        out_shape=jax.ShapeDtypeStruct((batch, seq, B_DIM), BF16),
        scratch_shapes=[pltpu.VMEM((nb, B_HEADS, B_HEAD_DIM, B_HEAD_DIM), F32)],
        compiler_params=_params("parallel", "arbitrary"),
        name="dn_scan",
    )(*[arr for arr, _ in ins], norm_g)
    return out.reshape(batch * seq, B_DIM)


def _attn_kernel(q_ref, k_ref, v_ref, bias_ref, o_ref):
    S = q_ref.shape[0]
    width = q_ref.shape[1]
    heads = width // C_HEAD_DIM
    head_of_lane = lax.broadcasted_iota(jnp.int32, (ATT_BLOCK, width), 1) // C_HEAD_DIM

    def key_chunks(i):
        nk = (i + 1) * ATT_BLOCK
        return [(k0, min(ATT_KEYS, nk - k0)) for k0 in range(0, nk, ATT_KEYS)]

    def scores(i, h):
        nk = (i + 1) * ATT_BLOCK
        q = q_ref[pl.ds(i * ATT_BLOCK, ATT_BLOCK), :]
        qh = jnp.where(head_of_lane == h, q, jnp.zeros_like(q))
        return [_dot_nt(qh, k_ref[pl.ds(k0, kw), :]) + bias_ref[:, pl.ds(S - nk + k0, kw)]
                for k0, kw in key_chunks(i)]

    def softmax_pv(i, s):
        mx = functools.reduce(jnp.maximum, [jnp.max(t, axis=-1, keepdims=True) for t in s])
        e = [jnp.exp(t - mx) for t in s]
        den = sum(jnp.sum(t, axis=-1, keepdims=True) for t in e)
        acc = sum(_dot(t.astype(BF16), v_ref[pl.ds(k0, kw), :]) for t, (k0, kw) in zip(e, key_chunks(i)))
        return acc / den

    tasks = [(i, h) for i in range(S // ATT_BLOCK) for h in range(heads)]
    s_next = scores(*tasks[0])
    out = None
    for t, (i, h) in enumerate(tasks):
        s_cur = s_next
        if t + 1 < len(tasks):
            s_next = scores(*tasks[t + 1])
        pv = softmax_pv(i, s_cur)
        out = pv if h == 0 else jnp.where(head_of_lane == h, pv, out)
        if h == heads - 1:
            o_ref[pl.ds(i * ATT_BLOCK, ATT_BLOCK), :] = out.astype(o_ref.dtype)


def _attn_bias(seq):
    r = np.arange(ATT_BLOCK)[:, None]
    c = np.arange(seq)[None, :]
    dist = (seq - ATT_BLOCK) + r - c
    mult = np.zeros(dist.shape, np.float64)
    for window, dil in C_CONFIGS:
        mult += (dist >= 0) & (dist <= window) & (dist % dil == 0)
    with np.errstate(divide="ignore"):
        return jnp.asarray(np.log(mult), F32)


def _attention(cq, ck, cv, bias, batch, seq):
    width = 2 * C_HEAD_DIM
    spec = pl.BlockSpec((seq, width), lambda b, p: (b, p))
    return pl.pallas_call(
        _attn_kernel,
        grid=(batch, C_DIM // width),
        in_specs=[spec, spec, spec, _const_spec(bias.shape)],
        out_specs=spec,
        out_shape=jax.ShapeDtypeStruct((batch * seq, C_DIM), BF16),
        compiler_params=_params("parallel", "parallel"),
        name="dilated_attn",
    )(cq, ck, cv, bias)


def _outproj_kernel(x_ref, ya_ref, yb_ref, yc_ref, wa_ref, wb_ref, wc_ref, g_ref, b_ref, o_ref):
    mix = _dot(ya_ref[...], wa_ref[...]) + _dot(yb_ref[...], wb_ref[...]) + _dot(yc_ref[...], wc_ref[...])
    o_ref[...] = _layer_norm(DN_ALPHA * x_ref[...] + mix, g_ref[...], b_ref[...])


def _outproj(x2d, ya, yb, yc, wa, wb, wc, g, b):
    T = x2d.shape[0]
    tm = ROW_TILE
    row = lambda n: pl.BlockSpec((tm, n), lambda i: (i, 0))
    return pl.pallas_call(
        _outproj_kernel,
        grid=(T // tm,),
        in_specs=[row(D_MODEL), row(A_DIM), row(B_DIM), row(C_DIM), _const_spec(wa.shape),
                  _const_spec(wb.shape), _const_spec(wc.shape), _const_spec(g.shape), _const_spec(b.shape)],
        out_specs=row(D_MODEL),
        out_shape=jax.ShapeDtypeStruct((T, D_MODEL), F32),
        compiler_params=_params("parallel"),
        name="outproj_ln",
    )(x2d, ya, yb, yc, wa, wb, wc, g, b)


def _swiglu(xb, wg_ref, wu_ref, wd_ref):
    slabs = [slice(c, c + FF_SUB) for c in range(0, D_FF, FF_SUB)]

    def gate_up(cols):
        return _dot(xb, wg_ref[0, :, cols]), _dot(xb, wu_ref[0, :, cols])

    acc = None
    nxt = gate_up(slabs[0])
    for c, cols in enumerate(slabs):
        gate, up = nxt
        if c + 1 < len(slabs):
            nxt = gate_up(slabs[c + 1])
        h = (gate * jax.nn.sigmoid(gate) * up).astype(BF16)
        part = _dot(h, wd_ref[0, cols, :])
        acc = part if acc is None else acc + part
    return acc


def _ffn_kernel(x_ref, wg_ref, wu_ref, wd_ref, g_ref, b_ref, o_ref):
    f = _swiglu(x_ref[...].astype(BF16), wg_ref, wu_ref, wd_ref)
    o_ref[...] = _layer_norm(DN_ALPHA * x_ref[...] + f, g_ref[...], b_ref[...])


def _ffn(x2d, wg, wu, wd, li, g, b):
    T = x2d.shape[0]
    tm = ROW_TILE
    return pl.pallas_call(
        _ffn_kernel,
        grid=(T // tm,),
        in_specs=[pl.BlockSpec((tm, D_MODEL), lambda i: (i, 0)),
                  pl.BlockSpec((1, D_MODEL, D_FF), lambda i: (li, 0, 0)),
                  pl.BlockSpec((1, D_MODEL, D_FF), lambda i: (li, 0, 0)),
                  pl.BlockSpec((1, D_FF, D_MODEL), lambda i: (li, 0, 0)),
                  _const_spec(g.shape), _const_spec(b.shape)],
        out_specs=pl.BlockSpec((tm, D_MODEL), lambda i: (i, 0)),
        out_shape=jax.ShapeDtypeStruct((T, D_MODEL), F32),
        compiler_params=_params("parallel"),
        name="ffn_ln",
    )(x2d, wg, wu, wd, g, b)


def _router_kernel(x_ref, w_ref, route_ref, counts_ref, carry_ref):
    i = pl.program_id(0)
    tm = x_ref.shape[0]

    @pl.when(i == 0)
    def _():
        carry_ref[...] = jnp.zeros_like(carry_ref)

    logits = _dot_narrow(x_ref[...], w_ref[...])
    lane = lax.broadcasted_iota(jnp.int32, (tm, LANES), 1)
    logits = jnp.where(lane < N_EXPERTS, logits, -jnp.inf)
    m1 = jnp.max(logits, axis=-1, keepdims=True)
    e1 = jnp.min(jnp.where(logits == m1, lane, LANES), axis=-1, keepdims=True)
    rest = jnp.where(lane == e1, -jnp.inf, logits)
    m2 = jnp.max(rest, axis=-1, keepdims=True)
    e2 = jnp.min(jnp.where(rest == m2, lane, LANES), axis=-1, keepdims=True)
    t = jnp.exp(m2 - m1)
    p1 = 1.0 / (1.0 + t)
    p2 = t / (1.0 + t)
    hot1 = lane == e1
    hot2 = lane == e2
    hot = jnp.where(hot1 | hot2, 1.0, 0.0)
    rr = lax.broadcasted_iota(jnp.int32, (tm, tm), 0)
    cc = lax.broadcasted_iota(jnp.int32, (tm, tm), 1)
    before = jnp.where(rr > cc, 1.0, 0.0).astype(BF16)
    cnt = _dot(before, hot.astype(BF16)) + carry_ref[0:1, :]
    rank1 = jnp.sum(jnp.where(hot1, cnt, 0.0), axis=-1, keepdims=True)
    rank2 = jnp.sum(jnp.where(hot2, cnt, 0.0), axis=-1, keepdims=True)
    route = jnp.where(lane == 0, e1.astype(F32), 0.0)
    route = jnp.where(lane == 1, e2.astype(F32), route)
    route = jnp.where(lane == 2, p1, route)
    route = jnp.where(lane == 3, p2, route)
    route = jnp.where(lane == 4, rank1, route)
    route = jnp.where(lane == 5, rank2, route)
    route_ref[...] = route
    carry_ref[...] = carry_ref[...] + jnp.sum(hot, axis=0, keepdims=True)
    counts_ref[...] = carry_ref[...]


def _router(x2d, w_pad):
    T = x2d.shape[0]
    tm = ROW_TILE
    return pl.pallas_call(
        _router_kernel,
        grid=(T // tm,),
        in_specs=[pl.BlockSpec((tm, D_MODEL), lambda i: (i, 0)), _const_spec(w_pad.shape)],
        out_specs=[pl.BlockSpec((tm, LANES), lambda i: (i, 0)), _const_spec((8, LANES))],
        out_shape=[jax.ShapeDtypeStruct((T, LANES), F32), jax.ShapeDtypeStruct((8, LANES), F32)],
        scratch_shapes=[pltpu.VMEM((8, LANES), F32)],
        compiler_params=_params("arbitrary"),
        name="moe_router",
    )(x2d, w_pad)


def _start_row_copies(n, make_copies):
    def step(g, c):
        for j in range(DMA_UNROLL):
            for idx, cp in enumerate(make_copies(g * DMA_UNROLL + j)):
                cp.start(priority=(j + idx) % 2)
        return c

    lax.fori_loop(0, n // DMA_UNROLL, step, 0)


def _dispatch_kernel(pos0_ref, pos1_ref, pad_ref, x_ref, xs_ref, zero_ref, sem, zero_sem):
    n = x_ref.shape[0]

    @pl.when(pl.program_id(0) == 0)
    def _():
        zero_ref[...] = jnp.zeros_like(zero_ref)

        def zero_copy(row):
            return pltpu.make_async_copy(zero_ref.at[pl.ds(0, 1), :], xs_ref.at[pl.ds(row, 1), :], zero_sem)

        for e in range(pad_ref.shape[1]):
            def start(r, c, e=e):
                zero_copy(pad_ref[0, e] + r).start()
                return c

            def wait(r, c, e=e):
                zero_copy(pad_ref[0, e] + r).wait()
                return c

            lax.fori_loop(0, pad_ref[1, e], start, 0)
            lax.fori_loop(0, pad_ref[1, e], wait, 0)

    def copies(t):
        src = x_ref.at[pl.ds(t, 1), :]
        return (pltpu.make_async_copy(src, xs_ref.at[pl.ds(pos0_ref[0, 0, t], 1), :], sem),
                pltpu.make_async_copy(src, xs_ref.at[pl.ds(pos1_ref[0, 0, t], 1), :], sem))

    _start_row_copies(n, copies)
    for _ in range(2):
        pltpu.make_async_copy(x_ref, xs_ref.at[pl.ds(0, n), :], sem).wait()


def _dispatch(x2d, pos0, pos1, pads, n_rows):
    T = x2d.shape[0]
    td = DMA_ROWS
    idx = pl.BlockSpec((1, 1, td), lambda i: (i, 0, 0), memory_space=pltpu.SMEM)
    return pl.pallas_call(
        _dispatch_kernel,
        grid=(T // td,),
        in_specs=[idx, idx, pl.BlockSpec(memory_space=pltpu.SMEM),
                  pl.BlockSpec((td, D_MODEL), lambda i: (i, 0))],
        out_specs=pl.BlockSpec(memory_space=pl.ANY),
        out_shape=jax.ShapeDtypeStruct((n_rows, D_MODEL), F32),
        scratch_shapes=[pltpu.VMEM((8, D_MODEL), F32), pltpu.SemaphoreType.DMA(()),
                        pltpu.SemaphoreType.DMA(())],
        compiler_params=_params("arbitrary"),
        name="moe_dispatch",
    )(pos0.reshape(T // td, 1, td), pos1.reshape(T // td, 1, td), pads, x2d)


def _combine_kernel(pos0_ref, pos1_ref, x_ref, route_ref, ys_ref, g_ref, b_ref, o_ref, buf0, buf1, sem):
    n = x_ref.shape[0]

    def copies(t):
        return (pltpu.make_async_copy(ys_ref.at[pl.ds(pos0_ref[0, 0, t], 1), :], buf0.at[pl.ds(t, 1), :], sem),
                pltpu.make_async_copy(ys_ref.at[pl.ds(pos1_ref[0, 0, t], 1), :], buf1.at[pl.ds(t, 1), :], sem))

    _start_row_copies(n, copies)
    for buf in (buf0, buf1):
        pltpu.make_async_copy(ys_ref.at[pl.ds(0, n), :], buf, sem).wait()
    route = route_ref[...]
    f = route[:, 2:3] * buf0[...] + route[:, 3:4] * buf1[...]
    o_ref[...] = _layer_norm(DN_ALPHA * x_ref[...] + f, g_ref[...], b_ref[...])


def _combine(x2d, route, ys, pos0, pos1, g, b):
    T = x2d.shape[0]
    td = DMA_ROWS
    idx = pl.BlockSpec((1, 1, td), lambda i: (i, 0, 0), memory_space=pltpu.SMEM)
    return pl.pallas_call(
        _combine_kernel,
        grid=(T // td,),
        in_specs=[idx, idx, pl.BlockSpec((td, D_MODEL), lambda i: (i, 0)),
                  pl.BlockSpec((td, LANES), lambda i: (i, 0)),
                  pl.BlockSpec(memory_space=pl.ANY), _const_spec(g.shape), _const_spec(b.shape)],
        out_specs=pl.BlockSpec((td, D_MODEL), lambda i: (i, 0)),
        out_shape=jax.ShapeDtypeStruct((T, D_MODEL), F32),
        scratch_shapes=[pltpu.VMEM((td, D_MODEL), F32), pltpu.VMEM((td, D_MODEL), F32),
                        pltpu.SemaphoreType.DMA(())],
        compiler_params=_params("arbitrary"),
        name="moe_combine_ln",
    )(pos0.reshape(T // td, 1, td), pos1.reshape(T // td, 1, td), x2d, route, ys, g, b)


def _moe_ffn_kernel(te_ref, nt_ref, x_ref, wg_ref, wu_ref, wd_ref, o_ref):
    del te_ref
    used = pl.program_id(0) < nt_ref[0]

    @pl.when(used)
    def _():
        o_ref[...] = _swiglu(x_ref[...].astype(BF16), wg_ref, wu_ref, wd_ref)

    @pl.when(jnp.logical_not(used))
    def _():
        o_ref[...] = jnp.zeros_like(o_ref)


def _moe_ffn(xs, tile_expert, n_tiles, wg, wu, wd):
    n_rows = xs.shape[0]
    tm = MOE_TILE

    def last_used(i, nt):
        return jnp.minimum(i, nt[0] - 1)

    def weights(shape):
        return pl.BlockSpec((1,) + shape, lambda i, te, nt: (te[last_used(i, nt)], 0, 0))

    grid_spec = pltpu.PrefetchScalarGridSpec(
        num_scalar_prefetch=2,
        grid=(n_rows // tm,),
        in_specs=[pl.BlockSpec((tm, D_MODEL), lambda i, te, nt: (last_used(i, nt), 0)),
                  weights((D_MODEL, D_FF)), weights((D_MODEL, D_FF)), weights((D_FF, D_MODEL))],
        out_specs=pl.BlockSpec((tm, D_MODEL), lambda i, te, nt: (i, 0)),
    )
    return pl.pallas_call(
        _moe_ffn_kernel,
        grid_spec=grid_spec,
        out_shape=jax.ShapeDtypeStruct((n_rows, D_MODEL), F32),
        compiler_params=_params("arbitrary"),
        name="moe_ffn",
    )(tile_expert, n_tiles, xs, wg, wu, wd)


def _moe(x2d, w_router, wg, wu, wd, li, g, b):
    T = x2d.shape[0]
    tm = MOE_TILE
    n_tiles_max = (2 * T) // tm + N_EXPERTS
    n_rows = n_tiles_max * tm
    route, counts = _router(x2d, _pack_narrow(w_router))
    cnt = counts[0, :N_EXPERTS].astype(jnp.int32)
    padded = ((cnt + tm - 1) // tm) * tm
    ends = jnp.cumsum(padded)
    offs = ends - padded
    e = route[:, 0:2].astype(jnp.int32)
    rank = route[:, 4:6].astype(jnp.int32)
    pos = offs[e] + rank
    pos0, pos1 = pos[:, 0], pos[:, 1]
    tile_start = jnp.arange(n_tiles_max, dtype=jnp.int32) * tm
    tile_expert = jnp.minimum(jnp.sum(tile_start[:, None] >= ends[None, :], axis=1), N_EXPERTS - 1).astype(jnp.int32)
    n_tiles = (ends[-1:] // tm).astype(jnp.int32)
    pads = jnp.stack([jnp.append(offs + cnt, ends[-1]), jnp.append(padded - cnt, n_rows - ends[-1])]).astype(jnp.int32)
    xs = _dispatch(x2d, pos0, pos1, pads, n_rows)
    ys = _moe_ffn(xs, tile_expert + li * N_EXPERTS, n_tiles, wg, wu, wd)
    return _combine(x2d, route, ys, pos0, pos1, g, b)


def _rope_tables(seq):
    inv_freq = ROPE_THETA ** (-jnp.arange(0, C_HEAD_DIM, 2, dtype=F32) / C_HEAD_DIM)
    ang = jnp.arange(seq, dtype=F32)[:, None] * inv_freq[None, :]
    cos, sin = jnp.cos(ang), jnp.sin(ang)
    cos_h = jnp.concatenate([cos, cos], axis=-1)
    sin_h = jnp.concatenate([-sin, sin], axis=-1)
    return jnp.tile(cos_h, (1, C_HEADS)), jnp.tile(sin_h, (1, C_HEADS))


def kernel(x, w_in, conv_w, a_ln_g, a_ln_b, a_ws, a_bs, b_a_log, b_dt_bias, b_norm_g, w_out, ln1_g, ln1_b, ln2_g, ln2_b, ffn_w_gate, ffn_w_up, ffn_w_down, moe_router, moe_w_gate, moe_w_up, moe_w_down):
    batch, seq, _ = x.shape
    T = batch * seq
    cos_t, sin_t = _rope_tables(seq)
    att_bias = _attn_bias(seq)
    o_a, o_bq, o_bz, o_beta, o_cq = 0, 2 * A_DIM, 2 * A_DIM + 3 * B_DIM, 2 * A_DIM + 4 * B_DIM, 2 * A_DIM + 4 * B_DIM + 2 * B_HEADS
    row = lambda v: v.reshape(1, -1)
    h2d = x.reshape(T, D_MODEL)
    dense_w = [t.astype(BF16) for t in (ffn_w_gate, ffn_w_up, ffn_w_down)]
    moe_w = [t.astype(BF16).reshape((-1,) + t.shape[2:]) for t in (moe_w_gate, moe_w_up, moe_w_down)]
    for layer in range(DEPTH):
        w = w_in[layer]
        wa = w[:, o_a:o_bq].astype(BF16)
        wb = w[:, o_bq:o_bz].astype(BF16)
        wz = w[:, o_bz:o_beta].astype(BF16)
        ws = _pack_narrow(w[:, o_beta:o_cq])
        wc = w[:, o_cq:].astype(BF16)
        au, av, bq, bk, bv, zs, small, cq, ck, cv = _inproj(h2d, wa, wb, wz, wc, ws, conv_w[layer], cos_t, sin_t, seq)

        bias2d = jnp.repeat(a_bs[layer].T, A_DIM // A_GROUPS, axis=1)
        ya = _gmlp(au, av, row(a_ln_g[layer]), row(a_ln_b[layer]), a_ws[layer], bias2d)

        zeros_row = jnp.zeros((1, LANES), F32)
        alog_row = zeros_row.at[0, B_HEADS:2 * B_HEADS].set(b_a_log[layer])
        dtb_row = zeros_row.at[0, B_HEADS:2 * B_HEADS].set(b_dt_bias[layer])
        u, wv, qd, kt, acomp, gl = _dn_prep(bq, bk, bv, small, alog_row, dtb_row)
        yb = _dn_scan(u, wv, qd, kt, acomp, gl, zs, row(b_norm_g[layer]), batch, seq)

        yc = _attention(cq, ck, cv, att_bias, batch, seq)

        wo = w_out[layer].astype(BF16)
        h2d = _outproj(h2d, ya, yb, yc, wo[:A_DIM], wo[A_DIM:A_DIM + B_DIM], wo[A_DIM + B_DIM:],
                       row(ln1_g[layer]), row(ln1_b[layer]))
        i = layer // 2
        if layer % 2 == 0:
            h2d = _ffn(h2d, *dense_w, i, row(ln2_g[layer]), row(ln2_b[layer]))
        else:
            h2d = _moe(h2d, moe_router[i], *moe_w, i, row(ln2_g[layer]), row(ln2_b[layer]))
    return h2d.reshape(batch, seq, D_MODEL)
```

```python
import functools

import jax
import jax.numpy as jnp
import numpy as np
from jax import lax
from jax.experimental import pallas as pl
from jax.experimental.pallas import tpu as pltpu

F32 = jnp.float32
BF16 = jnp.bfloat16

D_MODEL = 1024
DEPTH = 4
A_DIM = 256
A_GROUPS = 4
A_CHUNK = 128
B_HEAD_DIM = 128
B_DIM = 512
B_HEADS = 4
B_CONV = 4
B_CHUNK = 64
C_HEAD_DIM = 64
C_DIM = 256
C_HEADS = 4
C_CONFIGS = ((128, 1), (512, 4), (2048, 16))
ROPE_THETA = 10000.0
D_FF = 2816
N_EXPERTS = 8
DN_ALPHA = (2.0 * DEPTH) ** 0.25
LN_EPS = 1e-5

LANES = 128
VMEM_LIMIT = 56 * 1024 * 1024
ROW_TILE = 512
ATT_BLOCK = 128
ATT_KEYS = 256
DN_BLOCK = 2 * B_CHUNK
DN_SCAN_SEQS = 2
DN_SCAN_ROWS = 1024
MOE_TILE = 512
FF_SUB = 256
DMA_ROWS = 256
DMA_UNROLL = 8
ZERO_BIG = 64
ZERO_SMALL = 8


def _params(*sem):
    return pltpu.CompilerParams(dimension_semantics=sem, vmem_limit_bytes=VMEM_LIMIT)


def _dot(a, b):
    return jnp.dot(a, b, preferred_element_type=F32)


def _dot_nt(a, b):
    return lax.dot_general(a, b, (((1,), (1,)), ((), ())), preferred_element_type=F32)


def _split3(x):
    x1 = x.astype(BF16)
    r = x - x1.astype(F32)
    x2 = r.astype(BF16)
    x3 = (r - x2.astype(F32)).astype(BF16)
    return x1, x2, x3


NARROW = 8


def _pack_narrow(w):
    pieces = _split3(w)
    packed = jnp.zeros((w.shape[0], LANES), BF16)
    for i, piece in enumerate(pieces):
        packed = packed.at[:, i * NARROW:(i + 1) * NARROW].set(piece)
    return packed


def _dot_narrow(x, w_packed):
    x1 = x.astype(BF16)
    x2 = (x - x1.astype(F32)).astype(BF16)
    p = _dot(x1, w_packed) + _dot(x2, w_packed)
    p = p + pltpu.roll(p, LANES - NARROW, 1) + pltpu.roll(p, LANES - 2 * NARROW, 1)
    lane = lax.broadcasted_iota(jnp.int32, p.shape, 1)
    return jnp.where(lane < NARROW, p, 0.0)


def _gelu_tanh(x):
    z = np.sqrt(2.0 / np.pi) * (x + 0.044715 * (x * x * x))
    return x * jax.nn.sigmoid(2.0 * z)


def _layer_norm(y, g, b):
    mu = jnp.mean(y, axis=-1, keepdims=True)
    yc = y - mu
    var = jnp.mean(yc * yc, axis=-1, keepdims=True)
    return yc * lax.rsqrt(var + LN_EPS) * g + b


def _const_spec(shape):
    nd = len(shape)
    return pl.BlockSpec(shape, lambda *_: (0,) * nd)


def _inproj_kernel(x_ref, xp_ref, wa_ref, wb_ref, wz_ref, wc_ref, ws_ref, cw_ref, cos_ref, sin_ref,
                   au_ref, av_ref, bq_ref, bk_ref, bv_ref, bz_ref, small_ref, cq_ref, ck_ref, cv_ref, *, spt):
    tm = x_ref.shape[0]
    x = x_ref[...]
    xb = x.astype(BF16)
    a = _gelu_tanh(_dot(xb, wa_ref[...]))
    au_ref[...] = a[:, :A_DIM]
    av_ref[...] = a[:, A_DIM:]

    seq_start = (pl.program_id(0) % spt) == 0
    xpb = jnp.where(seq_start, 0.0, xp_ref[...]).astype(BF16)
    dk = B_HEAD_DIM
    for part, out_ref in enumerate((bq_ref, bk_ref, bv_ref)):
        cols = slice(part * B_DIM, (part + 1) * B_DIM)
        w = wb_ref[:, cols]
        cur = _dot(xb, w)
        ext = jnp.concatenate([_dot(xpb, w), cur], axis=0)
        cw = cw_ref[:, cols]
        acc = cur * cw[B_CONV - 1:B_CONV, :]
        for s in range(1, B_CONV):
            acc = acc + pltpu.roll(ext, s, 0)[8:, :] * cw[B_CONV - 1 - s:B_CONV - s, :]
        y = acc * jax.nn.sigmoid(acc)
        if part < 2:
            scale = dk ** -0.5 if part == 0 else 1.0
            segs = []
            for h in range(B_HEADS):
                seg = y[:, h * dk:(h + 1) * dk]
                segs.append(seg * (lax.rsqrt(jnp.sum(seg * seg, -1, keepdims=True) + 1e-6) * scale))
            y = jnp.concatenate(segs, axis=1)
        out_ref[...] = y.astype(out_ref.dtype)
    z = _dot(xb, wz_ref[...])
    bz_ref[...] = (z * jax.nn.sigmoid(z)).astype(bz_ref.dtype)
    small_ref[...] = _dot_narrow(x, ws_ref[...])

    c = _dot(xb, wc_ref[...])
    cos = cos_ref[...]
    sin = sin_ref[...]
    lane = lax.broadcasted_iota(jnp.int32, (tm, C_DIM), 1)
    first_half = (lane % C_HEAD_DIM) < (C_HEAD_DIM // 2)
    half = C_HEAD_DIM // 2

    def rope(t):
        swapped = jnp.where(first_half, pltpu.roll(t, C_DIM - half, 1), pltpu.roll(t, half, 1))
        return t * cos + swapped * sin

    cq_ref[...] = (rope(c[:, :C_DIM]) * (C_HEAD_DIM ** -0.5)).astype(BF16)
    ck_ref[...] = rope(c[:, C_DIM:2 * C_DIM]).astype(BF16)
    cv_ref[...] = c[:, 2 * C_DIM:].astype(BF16)


def _inproj(x2d, wa, wb, wz, wc, ws, conv_w, cos_t, sin_t, seq):
    T = x2d.shape[0]
    tm = ROW_TILE
    spt = seq // tm
    row = lambda n: pl.BlockSpec((tm, n), lambda i: (i, 0))
    prev = pl.BlockSpec((8, D_MODEL), lambda i: (jnp.maximum(i * (tm // 8) - 1, 0), 0))
    pos = pl.BlockSpec((tm, C_DIM), lambda i: (i % spt, 0))
    outs = [(A_DIM, F32), (A_DIM, F32), (B_DIM, BF16), (B_DIM, BF16), (B_DIM, BF16), (B_DIM, BF16),
            (LANES, F32), (C_DIM, BF16), (C_DIM, BF16), (C_DIM, BF16)]
    return pl.pallas_call(
        functools.partial(_inproj_kernel, spt=spt),
        grid=(T // tm,),
        in_specs=[row(D_MODEL), prev, _const_spec(wa.shape), _const_spec(wb.shape), _const_spec(wz.shape),
                  _const_spec(wc.shape), _const_spec(ws.shape), _const_spec(conv_w.shape), pos, pos],
        out_specs=[row(n) for n, _ in outs],
        out_shape=[jax.ShapeDtypeStruct((T, n), dt) for n, dt in outs],
        compiler_params=_params("parallel"),
        name="inproj",
    )(x2d, x2d, wa, wb, wz, wc, ws, conv_w, cos_t, sin_t)


def _gmlp_kernel(u_ref, v_ref, g_ref, b_ref, ws_ref, bias_ref, o_ref):
    n = u_ref.shape[0] // A_CHUNK
    r = lax.broadcasted_iota(jnp.int32, (A_CHUNK, A_CHUNK), 0)
    c = lax.broadcasted_iota(jnp.int32, (A_CHUNK, A_CHUNK), 1)
    causal = r >= c
    group = lax.broadcasted_iota(jnp.int32, (A_CHUNK, A_DIM), 1) // (A_DIM // A_GROUPS)
    ws = [jnp.where(causal, ws_ref[g], 0.0).astype(BF16) for g in range(A_GROUPS)]
    bias = bias_ref[...]
    for i in range(n):
        rows = pl.ds(i * A_CHUNK, A_CHUNK)
        vn = _layer_norm(v_ref[rows, :], g_ref[...], b_ref[...]).astype(BF16)
        mixed = bias
        for g in range(A_GROUPS):
            mixed = mixed + jnp.where(group == g, _dot(ws[g], vn), 0.0)
        o_ref[rows, :] = (u_ref[rows, :] * mixed).astype(o_ref.dtype)


def _gmlp(au, av, ln_g, ln_b, ws, bias2d):
    T = au.shape[0]
    tm = ROW_TILE
    row = pl.BlockSpec((tm, A_DIM), lambda i: (i, 0))
    return pl.pallas_call(
        _gmlp_kernel,
        grid=(T // tm,),
        in_specs=[row, row, _const_spec(ln_g.shape), _const_spec(ln_b.shape),
                  _const_spec(ws.shape), _const_spec(bias2d.shape)],
        out_specs=row,
        out_shape=jax.ShapeDtypeStruct((T, A_DIM), BF16),
        compiler_params=_params("parallel"),
        name="gmlp",
    )(au, av, ln_g, ln_b, ws, bias2d)


def _unit_lower_inverse_minus_eye(ms, blk):
    def bf(t):
        return t.astype(BF16)

    xs = [-jnp.where(blk(16), m, 0.0) for m in ms]
    ps = xs
    ys = [_dot(bf(x), bf(x)) for x in xs]
    for step in range(3):
        ps = [p + y + _dot(bf(p), bf(y)) for p, y in zip(ps, ys)]
        if step < 2:
            ys = [_dot(bf(y), bf(y)) for y in ys]
    for size in (32, 64):
        off = blk(size) & jnp.logical_not(blk(size // 2))
        ls = [jnp.where(off, m, 0.0) for m in ms]
        qs = [l + _dot(bf(p), bf(l)) for p, l in zip(ps, ls)]
        ps = [p - (q + _dot(bf(q), bf(p))) for p, q in zip(ps, qs)]
    return ps


def _dn_prep_kernel(q_ref, k_ref, v_ref, s_ref, alog_ref, dtb_ref,
                    u_ref, w_ref, qd_ref, kt_ref, a_ref, gl_ref, *, heads):
    tm = q_ref.shape[0]
    dk = B_HEAD_DIM
    C = B_CHUNK
    hp = pl.program_id(1)

    r = lax.broadcasted_iota(jnp.int32, (DN_BLOCK, DN_BLOCK), 0)
    c = lax.broadcasted_iota(jnp.int32, (DN_BLOCK, DN_BLOCK), 1)

    def blk(size):
        return (r // size) == (c // size)

    lower = (r >= c) & blk(C)
    strict = (r > c) & blk(C)
    cum_mask = jnp.where(lower, 1.0, 0.0).astype(BF16)
    first_chunk = lax.broadcasted_iota(jnp.int32, (DN_BLOCK, dk), 0) < C
    lane = lax.broadcasted_iota(jnp.int32, (DN_BLOCK, LANES), 1)

    chains = [(b, h) for b in range(tm // DN_BLOCK) for h in range(heads)]
    rows = {b: slice(b * DN_BLOCK, (b + 1) * DN_BLOCK) for b, _ in chains}
    cols = {h: slice(h * dk, (h + 1) * dk) for _, h in chains}

    gates = {}
    for b in rows:
        small = s_ref[rows[b], :]
        gates[b] = (jax.nn.sigmoid(small), -jnp.exp(alog_ref[...]) * jax.nn.softplus(small + dtb_ref[...]))
    beta, gcol = [], []
    for b, h in chains:
        sig, g_all = gates[b]
        hh = hp * heads + h
        beta.append(jnp.sum(jnp.where(lane == hh, sig, 0.0), -1, keepdims=True))
        g = jnp.sum(jnp.where(lane == B_HEADS + hh, g_all, 0.0), -1, keepdims=True)
        g1, g2, g3 = _split3(jnp.broadcast_to(g, (DN_BLOCK, dk)))
        gcol.append(_dot(cum_mask, g1) + _dot(cum_mask, g2) + _dot(cum_mask, g3))

    q = [q_ref[rows[b], cols[h]].astype(F32) for b, h in chains]
    k = [k_ref[rows[b], cols[h]].astype(F32) for b, h in chains]
    kb = [ki * bi for ki, bi in zip(k, beta)]
    kk = [_dot_nt(kbi.astype(BF16), ki.astype(BF16)) for kbi, ki in zip(kb, k)]
    qk = [_dot_nt(qi.astype(BF16), ki.astype(BF16)) for qi, ki in zip(q, k)]
    decay = [jnp.exp(jnp.where(lower, gc - gc.T, -jnp.inf)) for gc in gcol]
    m = [jnp.where(strict, kki * d, 0.0) for kki, d in zip(kk, decay)]
    t_off = _unit_lower_inverse_minus_eye(m, blk)
    e_gc = [jnp.exp(gc) for gc in gcol]
    rhs = [jnp.concatenate([v_ref[rows[b], cols[h]].astype(F32) * bi, kbi * e], axis=1)
           for (b, h), bi, kbi, e in zip(chains, beta, kb, e_gc)]
    sol = [ri + _dot(t.astype(BF16), ri.astype(BF16)) for ri, t in zip(rhs, t_off)]

    for i, (b, h) in enumerate(chains):
        u_ref[rows[b], cols[h]] = sol[i][:, :dk]
        w_ref[rows[b], cols[h]] = sol[i][:, dk:].astype(BF16)
        qd_ref[rows[b], cols[h]] = (q[i] * e_gc[i]).astype(BF16)
        glast = jnp.where(first_chunk, gcol[i][C - 1:C, :], gcol[i][2 * C - 1:2 * C, :])
        kt_ref[b, h] = (k[i] * jnp.exp(glast - gcol[i])).T.astype(BF16)
        attn = jnp.where(lower, qk[i] * decay[i], 0.0)
        a_ref[rows[b], h * C:(h + 1) * C] = (attn[:, :C] + attn[:, C:]).astype(BF16)
        gl = jnp.exp(glast)
        gl_ref[16 * b:16 * b + 8, cols[h]] = gl[0:8, :]
        gl_ref[16 * b + 8:16 * b + 16, cols[h]] = gl[C:C + 8, :]


def _dn_prep(bq, bk, bv, small, alog_row, dtb_row):
    T = bq.shape[0]
    heads = 2
    tm = 2 * ROW_TILE
    wcols = heads * B_HEAD_DIM
    nb = tm // DN_BLOCK
    qkv = pl.BlockSpec((tm, wcols), lambda i, h: (i, h))
    return pl.pallas_call(
        functools.partial(_dn_prep_kernel, heads=heads),
        grid=(T // tm, B_HEADS // heads),
        in_specs=[qkv, qkv, qkv, pl.BlockSpec((tm, LANES), lambda i, h: (i, 0)),
                  _const_spec(alog_row.shape), _const_spec(dtb_row.shape)],
        out_specs=[qkv, qkv, qkv,
                   pl.BlockSpec((nb, heads, B_HEAD_DIM, DN_BLOCK), lambda i, h: (i, h, 0, 0)),
                   pl.BlockSpec((tm, heads * B_CHUNK), lambda i, h: (i, h)),
                   pl.BlockSpec((8 * tm // B_CHUNK, wcols), lambda i, h: (i, h))],
        out_shape=[jax.ShapeDtypeStruct((T, B_DIM), F32),
                   jax.ShapeDtypeStruct((T, B_DIM), BF16),
                   jax.ShapeDtypeStruct((T, B_DIM), BF16),
                   jax.ShapeDtypeStruct((T // DN_BLOCK, B_HEADS, B_HEAD_DIM, DN_BLOCK), BF16),
                   jax.ShapeDtypeStruct((T, B_HEADS * B_CHUNK), BF16),
                   jax.ShapeDtypeStruct((8 * T // B_CHUNK, B_DIM), F32)],
        compiler_params=_params("parallel", "parallel"),
        name="dn_prep",
    )(bq, bk, bv, small, alog_row, dtb_row)


def _dn_scan_kernel(u_ref, w_ref, qd_ref, kt_ref, a_ref, gl_ref, z_ref, ng_ref, o_ref, state_ref):
    nb, rows_per_step = u_ref.shape[0], u_ref.shape[1]
    dk = B_HEAD_DIM
    C = B_CHUNK

    @pl.when(pl.program_id(1) == 0)
    def _():
        state_ref[...] = jnp.zeros_like(state_ref)

    chains = [(s, h) for s in range(nb) for h in range(B_HEADS)]
    cols = [slice(h * dk, (h + 1) * dk) for h in range(B_HEADS)]

    def body(n, carry):
        for ci in range(2):
            rows = pl.ds(pl.multiple_of(n * DN_BLOCK + ci * C, C), C)
            glr = pl.ds(pl.multiple_of((2 * n + ci) * 8, 8), 8)
            states = [state_ref[s, h] for s, h in chains]
            r1 = [_dot(jnp.concatenate([w_ref[s, rows, cols[h]], qd_ref[s, rows, cols[h]]], axis=0),
                       st.astype(BF16)) for (s, h), st in zip(chains, states)]
            v_new = [(u_ref[s, rows, cols[h]] - r[:C]).astype(BF16) for (s, h), r in zip(chains, r1)]
            r2 = [_dot(jnp.concatenate([a_ref[s, rows, h * C:(h + 1) * C],
                                        kt_ref[s, n, h][:, ci * C:(ci + 1) * C]], axis=0), v)
                  for (s, h), v in zip(chains, v_new)]
            for i, (s, h) in enumerate(chains):
                state_ref[s, h] = states[i] * gl_ref[s, glr, cols[h]][0:1, :] + r2[i][C:]
                o = r1[i][C:] + r2[i][:C]
                o = o * lax.rsqrt(jnp.mean(o * o, -1, keepdims=True) + 1e-6) * ng_ref[...]
                o_ref[s, rows, cols[h]] = (o * z_ref[s, rows, cols[h]].astype(F32)).astype(o_ref.dtype)
        return carry

    lax.fori_loop(0, rows_per_step // DN_BLOCK, body, 0)


def _dn_scan(u, w, qd, kt, acomp, gl, zs, norm_g, batch, seq):
    nb = DN_SCAN_SEQS
    rows = DN_SCAN_ROWS

    def per_seq(arr, rows_of_block):
        arr = arr.reshape((batch, arr.shape[0] // batch) + arr.shape[1:])
        block = (nb, rows_of_block) + arr.shape[2:]
        return arr, pl.BlockSpec(block, lambda b, j: (b, j) + (0,) * (len(block) - 2))

    ins = [per_seq(u, rows), per_seq(w, rows), per_seq(qd, rows), per_seq(kt, rows // DN_BLOCK),
           per_seq(acomp, rows), per_seq(gl, 8 * rows // B_CHUNK), per_seq(zs, rows)]
    out = pl.pallas_call(
        _dn_scan_kernel,
        grid=(batch // nb, seq // rows),
        in_specs=[spec for _, spec in ins] + [_const_spec(norm_g.shape)],
        out_specs=pl.BlockSpec((nb, rows, B_DIM), lambda b, j: (b, j, 0)),
        out_shape=jax.ShapeDtypeStruct((batch, seq, B_DIM), BF16),
        scratch_shapes=[pltpu.VMEM((nb, B_HEADS, B_HEAD_DIM, B_HEAD_DIM), F32)],
        compiler_params=_params("parallel", "arbitrary"),
        name="dn_scan",
    )(*[arr for arr, _ in ins], norm_g)
    return out.reshape(batch * seq, B_DIM)


def _attn_kernel(q_ref, k_ref, v_ref, bias_ref, o_ref):
    S = q_ref.shape[0]
    width = q_ref.shape[1]
    heads = width // C_HEAD_DIM
    head_of_lane = lax.broadcasted_iota(jnp.int32, (ATT_BLOCK, width), 1) // C_HEAD_DIM

    def key_chunks(i):
        nk = (i + 1) * ATT_BLOCK
        return [(k0, min(ATT_KEYS, nk - k0)) for k0 in range(0, nk, ATT_KEYS)]

    def scores(i, h):
        nk = (i + 1) * ATT_BLOCK
        q = q_ref[pl.ds(i * ATT_BLOCK, ATT_BLOCK), :]
        qh = jnp.where(head_of_lane == h, q, jnp.zeros_like(q))
        return [_dot_nt(qh, k_ref[pl.ds(k0, kw), :]) + bias_ref[:, pl.ds(S - nk + k0, kw)]
                for k0, kw in key_chunks(i)]

    def softmax_pv(i, s):
        mx = functools.reduce(jnp.maximum, [jnp.max(t, axis=-1, keepdims=True) for t in s])
        e = [jnp.exp(t - mx) for t in s]
        den = sum(jnp.sum(t, axis=-1, keepdims=True) for t in e)
        acc = sum(_dot(t.astype(BF16), v_ref[pl.ds(k0, kw), :]) for t, (k0, kw) in zip(e, key_chunks(i)))
        return acc / den

    tasks = [(i, h) for i in range(S // ATT_BLOCK) for h in range(heads)]
    s_next = scores(*tasks[0])
    out = None
    for t, (i, h) in enumerate(tasks):
        s_cur = s_next
        if t + 1 < len(tasks):
            s_next = scores(*tasks[t + 1])
        pv = softmax_pv(i, s_cur)
        out = pv if h == 0 else jnp.where(head_of_lane == h, pv, out)
        if h == heads - 1:
            o_ref[pl.ds(i * ATT_BLOCK, ATT_BLOCK), :] = out.astype(o_ref.dtype)


def _attn_bias(seq):
    r = np.arange(ATT_BLOCK)[:, None]
    c = np.arange(seq)[None, :]
    dist = (seq - ATT_BLOCK) + r - c
    mult = np.zeros(dist.shape, np.float64)
    for window, dil in C_CONFIGS:
        mult += (dist >= 0) & (dist <= window) & (dist % dil == 0)
    with np.errstate(divide="ignore"):
        return jnp.asarray(np.log(mult), F32)


def _attention(cq, ck, cv, bias, batch, seq):
    width = 2 * C_HEAD_DIM
    spec = pl.BlockSpec((seq, width), lambda b, p: (b, p))
    return pl.pallas_call(
        _attn_kernel,
        grid=(batch, C_DIM // width),
        in_specs=[spec, spec, spec, _const_spec(bias.shape)],
        out_specs=spec,
        out_shape=jax.ShapeDtypeStruct((batch * seq, C_DIM), BF16),
        compiler_params=_params("parallel", "parallel"),
        name="dilated_attn",
    )(cq, ck, cv, bias)


def _outproj_kernel(x_ref, ya_ref, yb_ref, yc_ref, wa_ref, wb_ref, wc_ref, g_ref, b_ref, o_ref):
    mix = _dot(ya_ref[...], wa_ref[...]) + _dot(yb_ref[...], wb_ref[...]) + _dot(yc_ref[...], wc_ref[...])
    o_ref[...] = _layer_norm(DN_ALPHA * x_ref[...] + mix, g_ref[...], b_ref[...])


def _outproj(x2d, ya, yb, yc, wa, wb, wc, g, b):
    T = x2d.shape[0]
    tm = ROW_TILE
    row = lambda n: pl.BlockSpec((tm, n), lambda i: (i, 0))
    return pl.pallas_call(
        _outproj_kernel,
        grid=(T // tm,),
        in_specs=[row(D_MODEL), row(A_DIM), row(B_DIM), row(C_DIM), _const_spec(wa.shape),
                  _const_spec(wb.shape), _const_spec(wc.shape), _const_spec(g.shape), _const_spec(b.shape)],
        out_specs=row(D_MODEL),
        out_shape=jax.ShapeDtypeStruct((T, D_MODEL), F32),
        compiler_params=_params("parallel"),
        name="outproj_ln",
    )(x2d, ya, yb, yc, wa, wb, wc, g, b)


def _swiglu(xb, wg_ref, wu_ref, wd_ref):
    slabs = [slice(c, c + FF_SUB) for c in range(0, D_FF, FF_SUB)]

    def gate_up(cols):
        return _dot(xb, wg_ref[0, :, cols]), _dot(xb, wu_ref[0, :, cols])

    acc = None
    nxt = gate_up(slabs[0])
    for c, cols in enumerate(slabs):
        gate, up = nxt
        if c + 1 < len(slabs):
            nxt = gate_up(slabs[c + 1])
        h = (gate * jax.nn.sigmoid(gate) * up).astype(BF16)
        part = _dot(h, wd_ref[0, cols, :])
        acc = part if acc is None else acc + part
    return acc


def _ffn_kernel(x_ref, wg_ref, wu_ref, wd_ref, g_ref, b_ref, o_ref):
    f = _swiglu(x_ref[...].astype(BF16), wg_ref, wu_ref, wd_ref)
    o_ref[...] = _layer_norm(DN_ALPHA * x_ref[...] + f, g_ref[...], b_ref[...])


def _ffn(x2d, wg, wu, wd, li, g, b):
    T = x2d.shape[0]
    tm = ROW_TILE
    return pl.pallas_call(
        _ffn_kernel,
        grid=(T // tm,),
        in_specs=[pl.BlockSpec((tm, D_MODEL), lambda i: (i, 0)),
                  pl.BlockSpec((1, D_MODEL, D_FF), lambda i: (li, 0, 0)),
                  pl.BlockSpec((1, D_MODEL, D_FF), lambda i: (li, 0, 0)),
                  pl.BlockSpec((1, D_FF, D_MODEL), lambda i: (li, 0, 0)),
                  _const_spec(g.shape), _const_spec(b.shape)],
        out_specs=pl.BlockSpec((tm, D_MODEL), lambda i: (i, 0)),
        out_shape=jax.ShapeDtypeStruct((T, D_MODEL), F32),
        compiler_params=_params("parallel"),
        name="ffn_ln",
    )(x2d, wg, wu, wd, g, b)


def _router_kernel(x_ref, w_ref, route_ref, counts_ref, carry_ref):
    i = pl.program_id(0)
    tm = x_ref.shape[0]

    @pl.when(i == 0)
    def _():
        carry_ref[...] = jnp.zeros_like(carry_ref)

    logits = _dot_narrow(x_ref[...], w_ref[...])
    lane = lax.broadcasted_iota(jnp.int32, (tm, LANES), 1)
    logits = jnp.where(lane < N_EXPERTS, logits, -jnp.inf)
    m1 = jnp.max(logits, axis=-1, keepdims=True)
    e1 = jnp.min(jnp.where(logits == m1, lane, LANES), axis=-1, keepdims=True)
    rest = jnp.where(lane == e1, -jnp.inf, logits)
    m2 = jnp.max(rest, axis=-1, keepdims=True)
    e2 = jnp.min(jnp.where(rest == m2, lane, LANES), axis=-1, keepdims=True)
    t = jnp.exp(m2 - m1)
    p1 = 1.0 / (1.0 + t)
    p2 = t / (1.0 + t)
    hot1 = lane == e1
    hot2 = lane == e2
    hot = jnp.where(hot1 | hot2, 1.0, 0.0)
    rr = lax.broadcasted_iota(jnp.int32, (tm, tm), 0)
    cc = lax.broadcasted_iota(jnp.int32, (tm, tm), 1)
    before = jnp.where(rr > cc, 1.0, 0.0).astype(BF16)
    cnt = _dot(before, hot.astype(BF16)) + carry_ref[0:1, :]
    rank1 = jnp.sum(jnp.where(hot1, cnt, 0.0), axis=-1, keepdims=True)
    rank2 = jnp.sum(jnp.where(hot2, cnt, 0.0), axis=-1, keepdims=True)
    route = jnp.where(lane == 0, e1.astype(F32), 0.0)
    route = jnp.where(lane == 1, e2.astype(F32), route)
    route = jnp.where(lane == 2, p1, route)
    route = jnp.where(lane == 3, p2, route)
    route = jnp.where(lane == 4, rank1, route)
    route = jnp.where(lane == 5, rank2, route)
    route_ref[...] = route
    carry_ref[...] = carry_ref[...] + jnp.sum(hot, axis=0, keepdims=True)
    counts_ref[...] = carry_ref[...]


def _router(x2d, w_pad):
    T = x2d.shape[0]
    tm = ROW_TILE
    return pl.pallas_call(
        _router_kernel,
        grid=(T // tm,),
        in_specs=[pl.BlockSpec((tm, D_MODEL), lambda i: (i, 0)), _const_spec(w_pad.shape)],
        out_specs=[pl.BlockSpec((tm, LANES), lambda i: (i, 0)), _const_spec((8, LANES))],
        out_shape=[jax.ShapeDtypeStruct((T, LANES), F32), jax.ShapeDtypeStruct((8, LANES), F32)],
        scratch_shapes=[pltpu.VMEM((8, LANES), F32)],
        compiler_params=_params("arbitrary"),
        name="moe_router",
    )(x2d, w_pad)


def _start_row_copies(n, make_copies):
    def step(g, c):
        for j in range(DMA_UNROLL):
            for idx, cp in enumerate(make_copies(g * DMA_UNROLL + j)):
                cp.start(priority=(j + idx) % 2)
        return c

    lax.fori_loop(0, n // DMA_UNROLL, step, 0)


def _dispatch_kernel(pos0_ref, pos1_ref, pad_ref, x_ref, xs_ref, zero_ref, sem, zero_sem):
    n = x_ref.shape[0]

    @pl.when(pl.program_id(0) == 0)
    def _():
        zero_ref[...] = jnp.zeros_like(zero_ref)
        for e in range(pad_ref.shape[1]):
            row, n_rows, row2, n_big, n_small = [pad_ref[k, e] for k in range(5)]

            def copies(kind, i, row=row, row2=row2, n_big=n_big):
                size = (1, ZERO_BIG, ZERO_SMALL)[kind]
                start = (row + i, row2 + i * ZERO_BIG, row2 + n_big * ZERO_BIG + i * ZERO_SMALL)[kind]
                if kind > 0:
                    start = pl.multiple_of(start, ZERO_SMALL)
                return pltpu.make_async_copy(zero_ref.at[pl.ds(0, size), :], xs_ref.at[pl.ds(start, size), :], zero_sem)

            for kind, count in enumerate((n_rows, n_big, n_small)):
                lax.fori_loop(0, count, lambda i, c, kind=kind: (copies(kind, i).start(), c)[1], 0)
            for kind, count in enumerate((n_rows, n_big, n_small)):
                lax.fori_loop(0, count, lambda i, c, kind=kind: (copies(kind, i).wait(), c)[1], 0)

    def copies(t):
        src = x_ref.at[pl.ds(t, 1), :]
        return (pltpu.make_async_copy(src, xs_ref.at[pl.ds(pos0_ref[0, 0, t], 1), :], sem),
                pltpu.make_async_copy(src, xs_ref.at[pl.ds(pos1_ref[0, 0, t], 1), :], sem))

    _start_row_copies(n, copies)
    for _ in range(2):
        pltpu.make_async_copy(x_ref, xs_ref.at[pl.ds(0, n), :], sem).wait()


def _dispatch(x2d, pos0, pos1, pads, n_rows):
    T = x2d.shape[0]
    td = DMA_ROWS
    idx = pl.BlockSpec((1, 1, td), lambda i: (i, 0, 0), memory_space=pltpu.SMEM)
    return pl.pallas_call(
        _dispatch_kernel,
        grid=(T // td,),
        in_specs=[idx, idx, pl.BlockSpec(memory_space=pltpu.SMEM),
                  pl.BlockSpec((td, D_MODEL), lambda i: (i, 0))],
        out_specs=pl.BlockSpec(memory_space=pl.ANY),
        out_shape=jax.ShapeDtypeStruct((n_rows, D_MODEL), F32),
        scratch_shapes=[pltpu.VMEM((ZERO_BIG, D_MODEL), F32), pltpu.SemaphoreType.DMA(()),
                        pltpu.SemaphoreType.DMA(())],
        compiler_params=_params("arbitrary"),
        name="moe_dispatch",
    )(pos0.reshape(T // td, 1, td), pos1.reshape(T // td, 1, td), pads, x2d)


def _combine_kernel(pos0_ref, pos1_ref, next0_ref, next1_ref, x_ref, route_ref, ys_ref, g_ref, b_ref, o_ref,
                    buf0, buf1, sems):
    i = pl.program_id(0)
    n = x_ref.shape[0]
    slot = i % 2

    def start_gather(p0_ref, p1_ref, dst):
        def copies(t):
            return (pltpu.make_async_copy(ys_ref.at[pl.ds(p0_ref[0, 0, t], 1), :],
                                          buf0.at[dst, pl.ds(t, 1), :], sems.at[dst]),
                    pltpu.make_async_copy(ys_ref.at[pl.ds(p1_ref[0, 0, t], 1), :],
                                          buf1.at[dst, pl.ds(t, 1), :], sems.at[dst]))

        _start_row_copies(n, copies)

    @pl.when(i == 0)
    def _():
        start_gather(pos0_ref, pos1_ref, 0)

    @pl.when(i + 1 < pl.num_programs(0))
    def _():
        start_gather(next0_ref, next1_ref, 1 - slot)

    for buf in (buf0, buf1):
        pltpu.make_async_copy(ys_ref.at[pl.ds(0, n), :], buf.at[slot], sems.at[slot]).wait()
    route = route_ref[...]
    f = route[:, 2:3] * buf0[slot] + route[:, 3:4] * buf1[slot]
    o_ref[...] = _layer_norm(DN_ALPHA * x_ref[...] + f, g_ref[...], b_ref[...])


def _combine(x2d, route, ys, pos0, pos1, g, b):
    T = x2d.shape[0]
    td = DMA_ROWS
    steps = T // td
    idx = pl.BlockSpec((1, 1, td), lambda i: (i, 0, 0), memory_space=pltpu.SMEM)
    nxt = pl.BlockSpec((1, 1, td), lambda i: (jnp.minimum(i + 1, steps - 1), 0, 0), memory_space=pltpu.SMEM)
    rows = [pos0.reshape(steps, 1, td), pos1.reshape(steps, 1, td)]
    buf = pltpu.VMEM((2, td, D_MODEL), F32)
    return pl.pallas_call(
        _combine_kernel,
        grid=(steps,),
        in_specs=[idx, idx, nxt, nxt, pl.BlockSpec((td, D_MODEL), lambda i: (i, 0)),
                  pl.BlockSpec((td, LANES), lambda i: (i, 0)),
                  pl.BlockSpec(memory_space=pl.ANY), _const_spec(g.shape), _const_spec(b.shape)],
        out_specs=pl.BlockSpec((td, D_MODEL), lambda i: (i, 0)),
        out_shape=jax.ShapeDtypeStruct((T, D_MODEL), F32),
        scratch_shapes=[buf, buf, pltpu.SemaphoreType.DMA((2,))],
        compiler_params=_params("arbitrary"),
        name="moe_combine_ln",
    )(*rows, *rows, x2d, route, ys, g, b)


def _moe_ffn_kernel(te_ref, nt_ref, x_ref, wg_ref, wu_ref, wd_ref, o_ref):
    del te_ref
    used = pl.program_id(0) < nt_ref[0]

    @pl.when(used)
    def _():
        o_ref[...] = _swiglu(x_ref[...].astype(BF16), wg_ref, wu_ref, wd_ref)

    @pl.when(jnp.logical_not(used))
    def _():
        o_ref[...] = jnp.zeros_like(o_ref)


def _moe_ffn(xs, tile_expert, n_tiles, wg, wu, wd):
    n_rows = xs.shape[0]
    tm = MOE_TILE

    def last_used(i, nt):
        return jnp.minimum(i, nt[0] - 1)

    def weights(shape):
        return pl.BlockSpec((1,) + shape, lambda i, te, nt: (te[last_used(i, nt)], 0, 0))

    grid_spec = pltpu.PrefetchScalarGridSpec(
        num_scalar_prefetch=2,
        grid=(n_rows // tm,),
        in_specs=[pl.BlockSpec((tm, D_MODEL), lambda i, te, nt: (last_used(i, nt), 0)),
                  weights((D_MODEL, D_FF)), weights((D_MODEL, D_FF)), weights((D_FF, D_MODEL))],
        out_specs=pl.BlockSpec((tm, D_MODEL), lambda i, te, nt: (i, 0)),
    )
    return pl.pallas_call(
        _moe_ffn_kernel,
        grid_spec=grid_spec,
        out_shape=jax.ShapeDtypeStruct((n_rows, D_MODEL), F32),
        compiler_params=_params("arbitrary"),
        name="moe_ffn",
    )(tile_expert, n_tiles, xs, wg, wu, wd)


def _moe(x2d, w_router, wg, wu, wd, li, g, b):
    T = x2d.shape[0]
    tm = MOE_TILE
    n_tiles_max = (2 * T) // tm + N_EXPERTS
    n_rows = n_tiles_max * tm
    route, counts = _router(x2d, _pack_narrow(w_router))
    cnt = counts[0, :N_EXPERTS].astype(jnp.int32)
    padded = ((cnt + tm - 1) // tm) * tm
    ends = jnp.cumsum(padded)
    offs = ends - padded
    e = route[:, 0:2].astype(jnp.int32)
    rank = route[:, 4:6].astype(jnp.int32)
    pos = offs[e] + rank
    pos0, pos1 = pos[:, 0], pos[:, 1]
    tile_start = jnp.arange(n_tiles_max, dtype=jnp.int32) * tm
    tile_expert = jnp.minimum(jnp.sum(tile_start[:, None] >= ends[None, :], axis=1), N_EXPERTS - 1).astype(jnp.int32)
    n_tiles = (ends[-1:] // tm).astype(jnp.int32)
    gap_start = jnp.append(offs + cnt, ends[-1])
    gap_len = jnp.append(padded - cnt, n_rows - ends[-1])
    head = jnp.minimum(-gap_start % ZERO_SMALL, gap_len)
    rest = gap_len - head
    pads = jnp.stack([gap_start, head, gap_start + head, rest // ZERO_BIG,
                      rest % ZERO_BIG // ZERO_SMALL]).astype(jnp.int32)
    xs = _dispatch(x2d, pos0, pos1, pads, n_rows)
    ys = _moe_ffn(xs, tile_expert + li * N_EXPERTS, n_tiles, wg, wu, wd)
    return _combine(x2d, route, ys, pos0, pos1, g, b)


def _rope_tables(seq):
    inv_freq = ROPE_THETA ** (-jnp.arange(0, C_HEAD_DIM, 2, dtype=F32) / C_HEAD_DIM)
    ang = jnp.arange(seq, dtype=F32)[:, None] * inv_freq[None, :]
    cos, sin = jnp.cos(ang), jnp.sin(ang)
    cos_h = jnp.concatenate([cos, cos], axis=-1)
    sin_h = jnp.concatenate([-sin, sin], axis=-1)
    return jnp.tile(cos_h, (1, C_HEADS)), jnp.tile(sin_h, (1, C_HEADS))


def kernel(x, w_in, conv_w, a_ln_g, a_ln_b, a_ws, a_bs, b_a_log, b_dt_bias, b_norm_g, w_out, ln1_g, ln1_b, ln2_g, ln2_b, ffn_w_gate, ffn_w_up, ffn_w_down, moe_router, moe_w_gate, moe_w_up, moe_w_down):
    batch, seq, _ = x.shape
    T = batch * seq
    cos_t, sin_t = _rope_tables(seq)
    att_bias = _attn_bias(seq)
    o_a, o_bq, o_bz, o_beta, o_cq = 0, 2 * A_DIM, 2 * A_DIM + 3 * B_DIM, 2 * A_DIM + 4 * B_DIM, 2 * A_DIM + 4 * B_DIM + 2 * B_HEADS
    row = lambda v: v.reshape(1, -1)
    h2d = x.reshape(T, D_MODEL)
    dense_w = [t.astype(BF16) for t in (ffn_w_gate, ffn_w_up, ffn_w_down)]
    moe_w = [t.astype(BF16).reshape((-1,) + t.shape[2:]) for t in (moe_w_gate, moe_w_up, moe_w_down)]
    for layer in range(DEPTH):
        w = w_in[layer]
        wa = w[:, o_a:o_bq].astype(BF16)
        wb = w[:, o_bq:o_bz].astype(BF16)
        wz = w[:, o_bz:o_beta].astype(BF16)
        ws = _pack_narrow(w[:, o_beta:o_cq])
        wc = w[:, o_cq:].astype(BF16)
        au, av, bq, bk, bv, zs, small, cq, ck, cv = _inproj(h2d, wa, wb, wz, wc, ws, conv_w[layer], cos_t, sin_t, seq)

        bias2d = jnp.repeat(a_bs[layer].T, A_DIM // A_GROUPS, axis=1)
        ya = _gmlp(au, av, row(a_ln_g[layer]), row(a_ln_b[layer]), a_ws[layer], bias2d)

        zeros_row = jnp.zeros((1, LANES), F32)
        alog_row = zeros_row.at[0, B_HEADS:2 * B_HEADS].set(b_a_log[layer])
        dtb_row = zeros_row.at[0, B_HEADS:2 * B_HEADS].set(b_dt_bias[layer])
        u, wv, qd, kt, acomp, gl = _dn_prep(bq, bk, bv, small, alog_row, dtb_row)
        yb = _dn_scan(u, wv, qd, kt, acomp, gl, zs, row(b_norm_g[layer]), batch, seq)

        yc = _attention(cq, ck, cv, att_bias, batch, seq)

        wo = w_out[layer].astype(BF16)
        h2d = _outproj(h2d, ya, yb, yc, wo[:A_DIM], wo[A_DIM:A_DIM + B_DIM], wo[A_DIM + B_DIM:],
                       row(ln1_g[layer]), row(ln1_b[layer]))
        i = layer // 2
        if layer % 2 == 0:
            h2d = _ffn(h2d, *dense_w, i, row(ln2_g[layer]), row(ln2_b[layer]))
        else:
            h2d = _moe(h2d, moe_router[i], *moe_w, i, row(ln2_g[layer]), row(ln2_b[layer]))
    return h2d.reshape(batch, seq, D_MODEL)
```

```python
import functools

import jax
import jax.numpy as jnp
import numpy as np
from jax import lax
from jax.experimental import pallas as pl
from jax.experimental.pallas import tpu as pltpu

F32 = jnp.float32
BF16 = jnp.bfloat16

D_MODEL = 1024
DEPTH = 4
A_DIM = 256
A_GROUPS = 4
A_CHUNK = 128
B_HEAD_DIM = 128
B_DIM = 512
B_HEADS = 4
B_CONV = 4
B_CHUNK = 64
C_HEAD_DIM = 64
C_DIM = 256
C_HEADS = 4
C_CONFIGS = ((128, 1), (512, 4), (2048, 16))
ROPE_THETA = 10000.0
D_FF = 2816
N_EXPERTS = 8
DN_ALPHA = (2.0 * DEPTH) ** 0.25
LN_EPS = 1e-5
LOG2_E = 1.4426950408889634

LANES = 128
VMEM_LIMIT = 56 * 1024 * 1024
ROW_TILE = 512
ATT_BLOCK = 128
ATT_KEYS = 256
DN_BLOCK = 2 * B_CHUNK
DN_SCAN_SEQS = 2
DN_SCAN_ROWS = 1024
MOE_TILE = 512
FF_SUB = 256
DMA_ROWS = 512
DMA_UNROLL = 8
ZERO_BIG = 64
ZERO_SMALL = 8


def _params(*sem):
    return pltpu.CompilerParams(dimension_semantics=sem, vmem_limit_bytes=VMEM_LIMIT)


def _dot(a, b):
    return jnp.dot(a, b, preferred_element_type=F32)


def _dot_nt(a, b):
    return lax.dot_general(a, b, (((1,), (1,)), ((), ())), preferred_element_type=F32)


def _split3(x):
    x1 = x.astype(BF16)
    r = x - x1.astype(F32)
    x2 = r.astype(BF16)
    x3 = (r - x2.astype(F32)).astype(BF16)
    return x1, x2, x3


NARROW = 8


def _pack_narrow(w):
    pieces = _split3(w)
    packed = jnp.zeros((w.shape[0], LANES), BF16)
    for i, piece in enumerate(pieces):
        packed = packed.at[:, i * NARROW:(i + 1) * NARROW].set(piece)
    return packed


def _dot_narrow(x, w_packed):
    x1 = x.astype(BF16)
    x2 = (x - x1.astype(F32)).astype(BF16)
    p = _dot(x1, w_packed) + _dot(x2, w_packed)
    p = p + pltpu.roll(p, LANES - NARROW, 1) + pltpu.roll(p, LANES - 2 * NARROW, 1)
    lane = lax.broadcasted_iota(jnp.int32, p.shape, 1)
    return jnp.where(lane < NARROW, p, 0.0)


def _gelu_tanh(x):
    z = np.sqrt(2.0 / np.pi) * (x + 0.044715 * (x * x * x))
    return x * jax.nn.sigmoid(2.0 * z)


def _layer_norm(y, g, b):
    mu = jnp.mean(y, axis=-1, keepdims=True)
    yc = y - mu
    var = jnp.mean(yc * yc, axis=-1, keepdims=True)
    return yc * lax.rsqrt(var + LN_EPS) * g + b


def _const_spec(shape):
    nd = len(shape)
    return pl.BlockSpec(shape, lambda *_: (0,) * nd)


def _inproj_kernel(x_ref, xp_ref, wa_ref, wb_ref, wz_ref, wc_ref, ws_ref, cw_ref, cos_ref, sin_ref,
                   au_ref, av_ref, bq_ref, bk_ref, bv_ref, bz_ref, small_ref, cq_ref, ck_ref, cv_ref, *, spt):
    tm = x_ref.shape[0]
    x = x_ref[...]
    xb = x.astype(BF16)
    a = _gelu_tanh(_dot(xb, wa_ref[...]))
    au_ref[...] = a[:, :A_DIM].astype(au_ref.dtype)
    av_ref[...] = a[:, A_DIM:].astype(av_ref.dtype)

    seq_start = (pl.program_id(0) % spt) == 0
    xpb = jnp.where(seq_start, 0.0, xp_ref[...]).astype(BF16)
    dk = B_HEAD_DIM
    for part, out_ref in enumerate((bq_ref, bk_ref, bv_ref)):
        cols = slice(part * B_DIM, (part + 1) * B_DIM)
        w = wb_ref[:, cols]
        cur = _dot(xb, w)
        ext = jnp.concatenate([_dot(xpb, w), cur], axis=0)
        cw = cw_ref[:, cols]
        acc = cur * cw[B_CONV - 1:B_CONV, :]
        for s in range(1, B_CONV):
            acc = acc + pltpu.roll(ext, s, 0)[8:, :] * cw[B_CONV - 1 - s:B_CONV - s, :]
        y = acc * jax.nn.sigmoid(acc)
        if part < 2:
            scale = dk ** -0.5 if part == 0 else 1.0
            segs = []
            for h in range(B_HEADS):
                seg = y[:, h * dk:(h + 1) * dk]
                segs.append(seg * (lax.rsqrt(jnp.sum(seg * seg, -1, keepdims=True) + 1e-6) * scale))
            y = jnp.concatenate(segs, axis=1)
        out_ref[...] = y.astype(out_ref.dtype)
    z = _dot(xb, wz_ref[...])
    bz_ref[...] = (z * jax.nn.sigmoid(z)).astype(bz_ref.dtype)
    small_ref[...] = _dot_narrow(x, ws_ref[...])

    c = _dot(xb, wc_ref[...])
    cos = cos_ref[...]
    sin = sin_ref[...]
    lane = lax.broadcasted_iota(jnp.int32, (tm, C_DIM), 1)
    first_half = (lane % C_HEAD_DIM) < (C_HEAD_DIM // 2)
    half = C_HEAD_DIM // 2

    def rope(t):
        swapped = jnp.where(first_half, pltpu.roll(t, C_DIM - half, 1), pltpu.roll(t, half, 1))
        return t * cos + swapped * sin

    cq_ref[...] = (rope(c[:, :C_DIM]) * (C_HEAD_DIM ** -0.5 * LOG2_E)).astype(BF16)
    ck_ref[...] = rope(c[:, C_DIM:2 * C_DIM]).astype(BF16)
    cv_ref[...] = c[:, 2 * C_DIM:].astype(BF16)


def _inproj(x2d, wa, wb, wz, wc, ws, conv_w, cos_t, sin_t, seq):
    T = x2d.shape[0]
    tm = ROW_TILE
    spt = seq // tm
    row = lambda n: pl.BlockSpec((tm, n), lambda i: (i, 0))
    prev = pl.BlockSpec((8, D_MODEL), lambda i: (jnp.maximum(i * (tm // 8) - 1, 0), 0))
    pos = pl.BlockSpec((tm, C_DIM), lambda i: (i % spt, 0))
    outs = [(A_DIM, BF16), (A_DIM, BF16), (B_DIM, BF16), (B_DIM, BF16), (B_DIM, BF16), (B_DIM, BF16),
            (LANES, F32), (C_DIM, BF16), (C_DIM, BF16), (C_DIM, BF16)]
    return pl.pallas_call(
        functools.partial(_inproj_kernel, spt=spt),
        grid=(T // tm,),
        in_specs=[row(D_MODEL), prev, _const_spec(wa.shape), _const_spec(wb.shape), _const_spec(wz.shape),
                  _const_spec(wc.shape), _const_spec(ws.shape), _const_spec(conv_w.shape), pos, pos],
        out_specs=[row(n) for n, _ in outs],
        out_shape=[jax.ShapeDtypeStruct((T, n), dt) for n, dt in outs],
        compiler_params=_params("parallel"),
        name="inproj",
    )(x2d, x2d, wa, wb, wz, wc, ws, conv_w, cos_t, sin_t)


def _gmlp_kernel(u_ref, v_ref, g_ref, b_ref, ws_ref, bias_ref, o_ref):
    n = u_ref.shape[0] // A_CHUNK
    r = lax.broadcasted_iota(jnp.int32, (A_CHUNK, A_CHUNK), 0)
    c = lax.broadcasted_iota(jnp.int32, (A_CHUNK, A_CHUNK), 1)
    causal = r >= c
    group = lax.broadcasted_iota(jnp.int32, (A_CHUNK, A_DIM), 1) // (A_DIM // A_GROUPS)
    ws = [jnp.where(causal, ws_ref[g], 0.0).astype(BF16) for g in range(A_GROUPS)]
    bias = bias_ref[...]
    for i in range(n):
        rows = pl.ds(i * A_CHUNK, A_CHUNK)
        vn = _layer_norm(v_ref[rows, :].astype(F32), g_ref[...], b_ref[...]).astype(BF16)
        mixed = bias
        for g in range(A_GROUPS):
            mixed = mixed + jnp.where(group == g, _dot(ws[g], vn), 0.0)
        o_ref[rows, :] = (u_ref[rows, :].astype(F32) * mixed).astype(o_ref.dtype)


def _gmlp(au, av, ln_g, ln_b, ws, bias2d):
    T = au.shape[0]
    tm = ROW_TILE
    row = pl.BlockSpec((tm, A_DIM), lambda i: (i, 0))
    return pl.pallas_call(
        _gmlp_kernel,
        grid=(T // tm,),
        in_specs=[row, row, _const_spec(ln_g.shape), _const_spec(ln_b.shape),
                  _const_spec(ws.shape), _const_spec(bias2d.shape)],
        out_specs=row,
        out_shape=jax.ShapeDtypeStruct((T, A_DIM), BF16),
        compiler_params=_params("parallel"),
        name="gmlp",
    )(au, av, ln_g, ln_b, ws, bias2d)


def _unit_lower_inverse_minus_eye(ms, blk):
    def bf(t):
        return t.astype(BF16)

    xs = [-jnp.where(blk(16), m, 0.0) for m in ms]
    ps = xs
    ys = [_dot(bf(x), bf(x)) for x in xs]
    for step in range(3):
        ps = [p + y + _dot(bf(p), bf(y)) for p, y in zip(ps, ys)]
        if step < 2:
            ys = [_dot(bf(y), bf(y)) for y in ys]
    for size in (32, 64):
        off = blk(size) & jnp.logical_not(blk(size // 2))
        ls = [jnp.where(off, m, 0.0) for m in ms]
        qs = [l + _dot(bf(p), bf(l)) for p, l in zip(ps, ls)]
        ps = [p - (q + _dot(bf(q), bf(p))) for p, q in zip(ps, qs)]
    return ps


def _dn_prep_kernel(q_ref, k_ref, v_ref, s_ref, alog_ref, dtb_ref,
                    u_ref, w_ref, qd_ref, kt_ref, a_ref, gl_ref, *, heads):
    tm = q_ref.shape[0]
    dk = B_HEAD_DIM
    C = B_CHUNK
    hp = pl.program_id(1)

    r = lax.broadcasted_iota(jnp.int32, (DN_BLOCK, DN_BLOCK), 0)
    c = lax.broadcasted_iota(jnp.int32, (DN_BLOCK, DN_BLOCK), 1)

    def blk(size):
        return (r // size) == (c // size)

    lower = (r >= c) & blk(C)
    strict = (r > c) & blk(C)
    cum_mask = jnp.where(lower, 1.0, 0.0).astype(BF16)
    first_chunk = lax.broadcasted_iota(jnp.int32, (DN_BLOCK, dk), 0) < C
    lane = lax.broadcasted_iota(jnp.int32, (DN_BLOCK, LANES), 1)

    chains = [(b, h) for b in range(tm // DN_BLOCK) for h in range(heads)]
    rows = {b: slice(b * DN_BLOCK, (b + 1) * DN_BLOCK) for b, _ in chains}
    cols = {h: slice(h * dk, (h + 1) * dk) for _, h in chains}

    gates = {}
    for b in rows:
        small = s_ref[rows[b], :]
        gates[b] = (jax.nn.sigmoid(small), -jnp.exp(alog_ref[...]) * jax.nn.softplus(small + dtb_ref[...]))
    beta, gcol = [], []
    for b, h in chains:
        sig, g_all = gates[b]
        hh = hp * heads + h
        beta.append(jnp.sum(jnp.where(lane == hh, sig, 0.0), -1, keepdims=True))
        g = jnp.sum(jnp.where(lane == B_HEADS + hh, g_all, 0.0), -1, keepdims=True)
        g1, g2, g3 = _split3(jnp.broadcast_to(g, (DN_BLOCK, dk)))
        gcol.append(_dot(cum_mask, g1) + _dot(cum_mask, g2) + _dot(cum_mask, g3))

    q = [q_ref[rows[b], cols[h]].astype(F32) for b, h in chains]
    k = [k_ref[rows[b], cols[h]].astype(F32) for b, h in chains]
    kb = [ki * bi for ki, bi in zip(k, beta)]
    kk = [_dot_nt(kbi.astype(BF16), ki.astype(BF16)) for kbi, ki in zip(kb, k)]
    qk = [_dot_nt(qi.astype(BF16), ki.astype(BF16)) for qi, ki in zip(q, k)]
    decay = [jnp.exp(jnp.where(lower, gc - gc.T, -jnp.inf)) for gc in gcol]
    m = [jnp.where(strict, kki * d, 0.0) for kki, d in zip(kk, decay)]
    t_off = _unit_lower_inverse_minus_eye(m, blk)
    e_gc = [jnp.exp(gc) for gc in gcol]
    rhs = [jnp.concatenate([v_ref[rows[b], cols[h]].astype(F32) * bi, kbi * e], axis=1)
           for (b, h), bi, kbi, e in zip(chains, beta, kb, e_gc)]
    sol = [ri + _dot(t.astype(BF16), ri.astype(BF16)) for ri, t in zip(rhs, t_off)]

    for i, (b, h) in enumerate(chains):
        u_ref[rows[b], cols[h]] = sol[i][:, :dk].astype(BF16)
        w_ref[rows[b], cols[h]] = sol[i][:, dk:].astype(BF16)
        qd_ref[rows[b], cols[h]] = (q[i] * e_gc[i]).astype(BF16)
        glast = jnp.where(first_chunk, gcol[i][C - 1:C, :], gcol[i][2 * C - 1:2 * C, :])
        kt_ref[b, h] = (k[i] * jnp.exp(glast - gcol[i])).T.astype(BF16)
        attn = jnp.where(lower, qk[i] * decay[i], 0.0)
        a_ref[rows[b], h * C:(h + 1) * C] = (attn[:, :C] + attn[:, C:]).astype(BF16)
        gl = jnp.exp(glast)
        gl_ref[16 * b:16 * b + 8, cols[h]] = gl[0:8, :]
        gl_ref[16 * b + 8:16 * b + 16, cols[h]] = gl[C:C + 8, :]


def _dn_prep(bq, bk, bv, small, alog_row, dtb_row):
    T = bq.shape[0]
    heads = 2
    tm = 2 * ROW_TILE
    wcols = heads * B_HEAD_DIM
    nb = tm // DN_BLOCK
    qkv = pl.BlockSpec((tm, wcols), lambda i, h: (i, h))
    return pl.pallas_call(
        functools.partial(_dn_prep_kernel, heads=heads),
        grid=(T // tm, B_HEADS // heads),
        in_specs=[qkv, qkv, qkv, pl.BlockSpec((tm, LANES), lambda i, h: (i, 0)),
                  _const_spec(alog_row.shape), _const_spec(dtb_row.shape)],
        out_specs=[qkv, qkv, qkv,
                   pl.BlockSpec((nb, heads, B_HEAD_DIM, DN_BLOCK), lambda i, h: (i, h, 0, 0)),
                   pl.BlockSpec((tm, heads * B_CHUNK), lambda i, h: (i, h)),
                   pl.BlockSpec((8 * tm // B_CHUNK, wcols), lambda i, h: (i, h))],
        out_shape=[jax.ShapeDtypeStruct((T, B_DIM), BF16),
                   jax.ShapeDtypeStruct((T, B_DIM), BF16),
                   jax.ShapeDtypeStruct((T, B_DIM), BF16),
                   jax.ShapeDtypeStruct((T // DN_BLOCK, B_HEADS, B_HEAD_DIM, DN_BLOCK), BF16),
                   jax.ShapeDtypeStruct((T, B_HEADS * B_CHUNK), BF16),
                   jax.ShapeDtypeStruct((8 * T // B_CHUNK, B_DIM), F32)],
        compiler_params=_params("parallel", "parallel"),
        name="dn_prep",
    )(bq, bk, bv, small, alog_row, dtb_row)


def _dn_scan_kernel(u_ref, w_ref, qd_ref, kt_ref, a_ref, gl_ref, z_ref, ng_ref, o_ref, state_ref):
    nb, rows_per_step = u_ref.shape[0], u_ref.shape[1]
    dk = B_HEAD_DIM
    C = B_CHUNK

    @pl.when(pl.program_id(1) == 0)
    def _():
        state_ref[...] = jnp.zeros_like(state_ref)

    chains = [(s, h) for s in range(nb) for h in range(B_HEADS)]
    cols = [slice(h * dk, (h + 1) * dk) for h in range(B_HEADS)]

    def body(n, carry):
        for ci in range(2):
            rows = pl.ds(pl.multiple_of(n * DN_BLOCK + ci * C, C), C)
            glr = pl.ds(pl.multiple_of((2 * n + ci) * 8, 8), 8)
            states = [state_ref[s, h] for s, h in chains]
            r1 = [_dot(jnp.concatenate([w_ref[s, rows, cols[h]], qd_ref[s, rows, cols[h]]], axis=0),
                       st.astype(BF16)) for (s, h), st in zip(chains, states)]
            v_new = [(u_ref[s, rows, cols[h]].astype(F32) - r[:C]).astype(BF16) for (s, h), r in zip(chains, r1)]
            r2 = [_dot(jnp.concatenate([a_ref[s, rows, h * C:(h + 1) * C],
                                        kt_ref[s, n, h][:, ci * C:(ci + 1) * C]], axis=0), v)
                  for (s, h), v in zip(chains, v_new)]
            for i, (s, h) in enumerate(chains):
                state_ref[s, h] = states[i] * gl_ref[s, glr, cols[h]][0:1, :] + r2[i][C:]
                o = r1[i][C:] + r2[i][:C]
                o = o * lax.rsqrt(jnp.mean(o * o, -1, keepdims=True) + 1e-6) * ng_ref[...]
                o_ref[s, rows, cols[h]] = (o * z_ref[s, rows, cols[h]].astype(F32)).astype(o_ref.dtype)
        return carry

    lax.fori_loop(0, rows_per_step // DN_BLOCK, body, 0)


def _dn_scan(u, w, qd, kt, acomp, gl, zs, norm_g, batch, seq):
    nb = DN_SCAN_SEQS
    rows = DN_SCAN_ROWS

    def per_seq(arr, rows_of_block):
        arr = arr.reshape((batch, arr.shape[0] // batch) + arr.shape[1:])
        block = (nb, rows_of_block) + arr.shape[2:]
        return arr, pl.BlockSpec(block, lambda b, j: (b, j) + (0,) * (len(block) - 2))

    ins = [per_seq(u, rows), per_seq(w, rows), per_seq(qd, rows), per_seq(kt, rows // DN_BLOCK),
           per_seq(acomp, rows), per_seq(gl, 8 * rows // B_CHUNK), per_seq(zs, rows)]
    out = pl.pallas_call(
        _dn_scan_kernel,
        grid=(batch // nb, seq // rows),
        in_specs=[spec for _, spec in ins] + [_const_spec(norm_g.shape)],
        out_specs=pl.BlockSpec((nb, rows, B_DIM), lambda b, j: (b, j, 0)),
        out_shape=jax.ShapeDtypeStruct((batch, seq, B_DIM), BF16),
        scratch_shapes=[pltpu.VMEM((nb, B_HEADS, B_HEAD_DIM, B_HEAD_DIM), F32)],
        compiler_params=_params("parallel", "arbitrary"),
        name="dn_scan",
    )(*[arr for arr, _ in ins], norm_g)
    return out.reshape(batch * seq, B_DIM)


def _attn_kernel(q_ref, k_ref, v_ref, bias_ref, o_ref):
    S = q_ref.shape[0]
    width = q_ref.shape[1]
    heads = width // C_HEAD_DIM
    head_of_lane = lax.broadcasted_iota(jnp.int32, (ATT_BLOCK, width), 1) // C_HEAD_DIM

    def key_chunks(i):
        nk = (i + 1) * ATT_BLOCK
        return [(k0, min(ATT_KEYS, nk - k0)) for k0 in range(0, nk, ATT_KEYS)]

    def scores(i, h):
        nk = (i + 1) * ATT_BLOCK
        q = q_ref[pl.ds(i * ATT_BLOCK, ATT_BLOCK), :]
        qh = jnp.where(head_of_lane == h, q, jnp.zeros_like(q))
        return [_dot_nt(qh, k_ref[pl.ds(k0, kw), :]) + bias_ref[:, pl.ds(S - nk + k0, kw)]
                for k0, kw in key_chunks(i)]

    def over_keys(op, lane_reduce, parts):
        by_width = {}
        for t in parts:
            by_width[t.shape[1]] = op(by_width[t.shape[1]], t) if t.shape[1] in by_width else t
        return functools.reduce(op, [lane_reduce(t, axis=-1, keepdims=True) for t in by_width.values()])

    def softmax_pv(i, s):
        mx = over_keys(jnp.maximum, jnp.max, s)
        e = [jnp.exp2(t - mx) for t in s]
        den = over_keys(jnp.add, jnp.sum, e)
        acc = sum(_dot(t.astype(BF16), v_ref[pl.ds(k0, kw), :]) for t, (k0, kw) in zip(e, key_chunks(i)))
        return acc / den

    tasks = [(i, h) for i in range(S // ATT_BLOCK) for h in range(heads)]
    s_next = scores(*tasks[0])
    out = None
    for t, (i, h) in enumerate(tasks):
        s_cur = s_next
        if t + 1 < len(tasks):
            s_next = scores(*tasks[t + 1])
        pv = softmax_pv(i, s_cur)
        out = pv if h == 0 else jnp.where(head_of_lane == h, pv, out)
        if h == heads - 1:
            o_ref[pl.ds(i * ATT_BLOCK, ATT_BLOCK), :] = out.astype(o_ref.dtype)


def _attn_bias(seq):
    r = np.arange(ATT_BLOCK)[:, None]
    c = np.arange(seq)[None, :]
    dist = (seq - ATT_BLOCK) + r - c
    mult = np.zeros(dist.shape, np.float64)
    for window, dil in C_CONFIGS:
        mult += (dist >= 0) & (dist <= window) & (dist % dil == 0)
    with np.errstate(divide="ignore"):
        return jnp.asarray(np.log2(mult), F32)


def _attention(cq, ck, cv, bias, batch, seq):
    width = 2 * C_HEAD_DIM
    spec = pl.BlockSpec((seq, width), lambda b, p: (b, p))
    return pl.pallas_call(
        _attn_kernel,
        grid=(batch, C_DIM // width),
        in_specs=[spec, spec, spec, _const_spec(bias.shape)],
        out_specs=spec,
        out_shape=jax.ShapeDtypeStruct((batch * seq, C_DIM), BF16),
        compiler_params=_params("parallel", "parallel"),
        name="dilated_attn",
    )(cq, ck, cv, bias)


def _outproj_kernel(x_ref, ya_ref, yb_ref, yc_ref, wa_ref, wb_ref, wc_ref, g_ref, b_ref, o_ref):
    mix = _dot(ya_ref[...], wa_ref[...]) + _dot(yb_ref[...], wb_ref[...]) + _dot(yc_ref[...], wc_ref[...])
    o_ref[...] = _layer_norm(DN_ALPHA * x_ref[...] + mix, g_ref[...], b_ref[...])


def _outproj(x2d, ya, yb, yc, wa, wb, wc, g, b):
    T = x2d.shape[0]
    tm = ROW_TILE
    row = lambda n: pl.BlockSpec((tm, n), lambda i: (i, 0))
    return pl.pallas_call(
        _outproj_kernel,
        grid=(T // tm,),
        in_specs=[row(D_MODEL), row(A_DIM), row(B_DIM), row(C_DIM), _const_spec(wa.shape),
                  _const_spec(wb.shape), _const_spec(wc.shape), _const_spec(g.shape), _const_spec(b.shape)],
        out_specs=row(D_MODEL),
        out_shape=jax.ShapeDtypeStruct((T, D_MODEL), F32),
        compiler_params=_params("parallel"),
        name="outproj_ln",
    )(x2d, ya, yb, yc, wa, wb, wc, g, b)


def _swiglu(xb, wg_ref, wu_ref, wd_ref):
    slabs = [slice(c, c + FF_SUB) for c in range(0, D_FF, FF_SUB)]

    def gate_up(cols):
        return _dot(xb, wg_ref[0, :, cols]), _dot(xb, wu_ref[0, :, cols])

    acc = None
    nxt = gate_up(slabs[0])
    for c, cols in enumerate(slabs):
        gate, up = nxt
        if c + 1 < len(slabs):
            nxt = gate_up(slabs[c + 1])
        h = (gate * jax.nn.sigmoid(gate) * up).astype(BF16)
        part = _dot(h, wd_ref[0, cols, :])
        acc = part if acc is None else acc + part
    return acc


def _ffn_kernel(x_ref, wg_ref, wu_ref, wd_ref, g_ref, b_ref, o_ref):
    f = _swiglu(x_ref[...].astype(BF16), wg_ref, wu_ref, wd_ref)
    o_ref[...] = _layer_norm(DN_ALPHA * x_ref[...] + f, g_ref[...], b_ref[...])


def _ffn(x2d, wg, wu, wd, li, g, b):
    T = x2d.shape[0]
    tm = ROW_TILE
    return pl.pallas_call(
        _ffn_kernel,
        grid=(T // tm,),
        in_specs=[pl.BlockSpec((tm, D_MODEL), lambda i: (i, 0)),
                  pl.BlockSpec((1, D_MODEL, D_FF), lambda i: (li, 0, 0)),
                  pl.BlockSpec((1, D_MODEL, D_FF), lambda i: (li, 0, 0)),
                  pl.BlockSpec((1, D_FF, D_MODEL), lambda i: (li, 0, 0)),
                  _const_spec(g.shape), _const_spec(b.shape)],
        out_specs=pl.BlockSpec((tm, D_MODEL), lambda i: (i, 0)),
        out_shape=jax.ShapeDtypeStruct((T, D_MODEL), F32),
        compiler_params=_params("parallel"),
        name="ffn_ln",
    )(x2d, wg, wu, wd, g, b)


def _router_kernel(x_ref, w_ref, route_ref, counts_ref, carry_ref):
    i = pl.program_id(0)
    tm = x_ref.shape[0]

    @pl.when(i == 0)
    def _():
        carry_ref[...] = jnp.zeros_like(carry_ref)

    logits = _dot_narrow(x_ref[...], w_ref[...])
    lane = lax.broadcasted_iota(jnp.int32, (tm, LANES), 1)
    logits = jnp.where(lane < N_EXPERTS, logits, -jnp.inf)
    m1 = jnp.max(logits, axis=-1, keepdims=True)
    e1 = jnp.min(jnp.where(logits == m1, lane, LANES), axis=-1, keepdims=True)
    rest = jnp.where(lane == e1, -jnp.inf, logits)
    m2 = jnp.max(rest, axis=-1, keepdims=True)
    e2 = jnp.min(jnp.where(rest == m2, lane, LANES), axis=-1, keepdims=True)
    t = jnp.exp(m2 - m1)
    p1 = 1.0 / (1.0 + t)
    p2 = t / (1.0 + t)
    hot1 = lane == e1
    hot2 = lane == e2
    hot = jnp.where(hot1 | hot2, 1.0, 0.0)
    rr = lax.broadcasted_iota(jnp.int32, (tm, tm), 0)
    cc = lax.broadcasted_iota(jnp.int32, (tm, tm), 1)
    before = jnp.where(rr > cc, 1.0, 0.0).astype(BF16)
    cnt = _dot(before, hot.astype(BF16)) + carry_ref[0:1, :]
    rank1 = jnp.sum(jnp.where(hot1, cnt, 0.0), axis=-1, keepdims=True)
    rank2 = jnp.sum(jnp.where(hot2, cnt, 0.0), axis=-1, keepdims=True)
    route = jnp.where(lane == 0, e1.astype(F32), 0.0)
    route = jnp.where(lane == 1, e2.astype(F32), route)
    route = jnp.where(lane == 2, p1, route)
    route = jnp.where(lane == 3, p2, route)
    route = jnp.where(lane == 4, rank1, route)
    route = jnp.where(lane == 5, rank2, route)
    route_ref[...] = route
    carry_ref[...] = carry_ref[...] + jnp.sum(hot, axis=0, keepdims=True)
    counts_ref[...] = carry_ref[...]


def _router(x2d, w_pad):
    T = x2d.shape[0]
    tm = ROW_TILE
    return pl.pallas_call(
        _router_kernel,
        grid=(T // tm,),
        in_specs=[pl.BlockSpec((tm, D_MODEL), lambda i: (i, 0)), _const_spec(w_pad.shape)],
        out_specs=[pl.BlockSpec((tm, LANES), lambda i: (i, 0)), _const_spec((8, LANES))],
        out_shape=[jax.ShapeDtypeStruct((T, LANES), F32), jax.ShapeDtypeStruct((8, LANES), F32)],
        scratch_shapes=[pltpu.VMEM((8, LANES), F32)],
        compiler_params=_params("arbitrary"),
        name="moe_router",
    )(x2d, w_pad)


def _start_row_copies(n, make_copies):
    def step(g, c):
        for j in range(DMA_UNROLL):
            for idx, cp in enumerate(make_copies(g * DMA_UNROLL + j)):
                cp.start(priority=(j + idx) % 2)
        return c

    lax.fori_loop(0, n // DMA_UNROLL, step, 0)


def _dispatch_kernel(pos0_ref, pos1_ref, pad_ref, x_ref, xs_ref, zero_ref, sem, zero_sem):
    n = x_ref.shape[0]

    @pl.when(pl.program_id(0) == 0)
    def _():
        zero_ref[...] = jnp.zeros_like(zero_ref)
        for e in range(pad_ref.shape[1]):
            row, n_rows, row2, n_big, n_small = [pad_ref[k, e] for k in range(5)]

            def copies(kind, i, row=row, row2=row2, n_big=n_big):
                size = (1, ZERO_BIG, ZERO_SMALL)[kind]
                start = (row + i, row2 + i * ZERO_BIG, row2 + n_big * ZERO_BIG + i * ZERO_SMALL)[kind]
                if kind > 0:
                    start = pl.multiple_of(start, ZERO_SMALL)
                return pltpu.make_async_copy(zero_ref.at[pl.ds(0, size), :], xs_ref.at[pl.ds(start, size), :], zero_sem)

            for kind, count in enumerate((n_rows, n_big, n_small)):
                lax.fori_loop(0, count, lambda i, c, kind=kind: (copies(kind, i).start(), c)[1], 0)
            for kind, count in enumerate((n_rows, n_big, n_small)):
                lax.fori_loop(0, count, lambda i, c, kind=kind: (copies(kind, i).wait(), c)[1], 0)

    def copies(t):
        src = x_ref.at[pl.ds(t, 1), :]
        return (pltpu.make_async_copy(src, xs_ref.at[pl.ds(pos0_ref[0, 0, t], 1), :], sem),
                pltpu.make_async_copy(src, xs_ref.at[pl.ds(pos1_ref[0, 0, t], 1), :], sem))

    _start_row_copies(n, copies)
    for _ in range(2):
        pltpu.make_async_copy(x_ref, xs_ref.at[pl.ds(0, n), :], sem).wait()


def _dispatch(x2d, pos0, pos1, pads, n_rows):
    T = x2d.shape[0]
    td = DMA_ROWS
    idx = pl.BlockSpec((1, 1, td), lambda i: (i, 0, 0), memory_space=pltpu.SMEM)
    return pl.pallas_call(
        _dispatch_kernel,
        grid=(T // td,),
        in_specs=[idx, idx, pl.BlockSpec(memory_space=pltpu.SMEM),
                  pl.BlockSpec((td, D_MODEL), lambda i: (i, 0))],
        out_specs=pl.BlockSpec(memory_space=pl.ANY),
        out_shape=jax.ShapeDtypeStruct((n_rows, D_MODEL), F32),
        scratch_shapes=[pltpu.VMEM((ZERO_BIG, D_MODEL), F32), pltpu.SemaphoreType.DMA(()),
                        pltpu.SemaphoreType.DMA(())],
        compiler_params=_params("arbitrary"),
        name="moe_dispatch",
    )(pos0.reshape(T // td, 1, td), pos1.reshape(T // td, 1, td), pads, x2d)


def _combine_kernel(pos0_ref, pos1_ref, next0_ref, next1_ref, x_ref, route_ref, ys_ref, g_ref, b_ref, o_ref,
                    buf0, buf1, sems):
    i = pl.program_id(0)
    n = x_ref.shape[0]
    slot = i % 2

    def start_gather(p0_ref, p1_ref, dst):
        def copies(t):
            return (pltpu.make_async_copy(ys_ref.at[pl.ds(p0_ref[0, 0, t], 1), :],
                                          buf0.at[dst, pl.ds(t, 1), :], sems.at[dst]),
                    pltpu.make_async_copy(ys_ref.at[pl.ds(p1_ref[0, 0, t], 1), :],
                                          buf1.at[dst, pl.ds(t, 1), :], sems.at[dst]))

        _start_row_copies(n, copies)

    @pl.when(i == 0)
    def _():
        start_gather(pos0_ref, pos1_ref, 0)

    @pl.when(i + 1 < pl.num_programs(0))
    def _():
        start_gather(next0_ref, next1_ref, 1 - slot)

    for buf in (buf0, buf1):
        pltpu.make_async_copy(ys_ref.at[pl.ds(0, n), :], buf.at[slot], sems.at[slot]).wait()
    route = route_ref[...]
    f = route[:, 2:3] * buf0[slot] + route[:, 3:4] * buf1[slot]
    o_ref[...] = _layer_norm(DN_ALPHA * x_ref[...] + f, g_ref[...], b_ref[...])


def _combine(x2d, route, ys, pos0, pos1, g, b):
    T = x2d.shape[0]
    td = DMA_ROWS
    steps = T // td
    idx = pl.BlockSpec((1, 1, td), lambda i: (i, 0, 0), memory_space=pltpu.SMEM)
    nxt = pl.BlockSpec((1, 1, td), lambda i: (jnp.minimum(i + 1, steps - 1), 0, 0), memory_space=pltpu.SMEM)
    rows = [pos0.reshape(steps, 1, td), pos1.reshape(steps, 1, td)]
    buf = pltpu.VMEM((2, td, D_MODEL), F32)
    return pl.pallas_call(
        _combine_kernel,
        grid=(steps,),
        in_specs=[idx, idx, nxt, nxt, pl.BlockSpec((td, D_MODEL), lambda i: (i, 0)),
                  pl.BlockSpec((td, LANES), lambda i: (i, 0)),
                  pl.BlockSpec(memory_space=pl.ANY), _const_spec(g.shape), _const_spec(b.shape)],
        out_specs=pl.BlockSpec((td, D_MODEL), lambda i: (i, 0)),
        out_shape=jax.ShapeDtypeStruct((T, D_MODEL), F32),
        scratch_shapes=[buf, buf, pltpu.SemaphoreType.DMA((2,))],
        compiler_params=_params("arbitrary"),
        name="moe_combine_ln",
    )(*rows, *rows, x2d, route, ys, g, b)


def _moe_ffn_kernel(te_ref, nt_ref, x_ref, wg_ref, wu_ref, wd_ref, o_ref):
    del te_ref
    used = pl.program_id(0) < nt_ref[0]

    @pl.when(used)
    def _():
        o_ref[...] = _swiglu(x_ref[...].astype(BF16), wg_ref, wu_ref, wd_ref)

    @pl.when(jnp.logical_not(used))
    def _():
        o_ref[...] = jnp.zeros_like(o_ref)


def _moe_ffn(xs, tile_expert, n_tiles, wg, wu, wd):
    n_rows = xs.shape[0]
    tm = MOE_TILE

    def last_used(i, nt):
        return jnp.minimum(i, nt[0] - 1)

    def weights(shape):
        return pl.BlockSpec((1,) + shape, lambda i, te, nt: (te[last_used(i, nt)], 0, 0))

    grid_spec = pltpu.PrefetchScalarGridSpec(
        num_scalar_prefetch=2,
        grid=(n_rows // tm,),
        in_specs=[pl.BlockSpec((tm, D_MODEL), lambda i, te, nt: (last_used(i, nt), 0)),
                  weights((D_MODEL, D_FF)), weights((D_MODEL, D_FF)), weights((D_FF, D_MODEL))],
        out_specs=pl.BlockSpec((tm, D_MODEL), lambda i, te, nt: (i, 0)),
    )
    return pl.pallas_call(
        _moe_ffn_kernel,
        grid_spec=grid_spec,
        out_shape=jax.ShapeDtypeStruct((n_rows, D_MODEL), F32),
        compiler_params=_params("arbitrary"),
        name="moe_ffn",
    )(tile_expert, n_tiles, xs, wg, wu, wd)


def _moe(x2d, w_router, wg, wu, wd, li, g, b):
    T = x2d.shape[0]
    tm = MOE_TILE
    n_tiles_max = (2 * T) // tm + N_EXPERTS
    n_rows = n_tiles_max * tm
    route, counts = _router(x2d, _pack_narrow(w_router))
    cnt = counts[0, :N_EXPERTS].astype(jnp.int32)
    padded = ((cnt + tm - 1) // tm) * tm
    ends = jnp.cumsum(padded)
    offs = ends - padded
    e = route[:, 0:2].astype(jnp.int32)
    rank = route[:, 4:6].astype(jnp.int32)
    pos = offs[e] + rank
    pos0, pos1 = pos[:, 0], pos[:, 1]
    tile_start = jnp.arange(n_tiles_max, dtype=jnp.int32) * tm
    tile_expert = jnp.minimum(jnp.sum(tile_start[:, None] >= ends[None, :], axis=1), N_EXPERTS - 1).astype(jnp.int32)
    n_tiles = (ends[-1:] // tm).astype(jnp.int32)
    gap_start = jnp.append(offs + cnt, ends[-1])
    gap_len = jnp.append(padded - cnt, n_rows - ends[-1])
    head = jnp.minimum(-gap_start % ZERO_SMALL, gap_len)
    rest = gap_len - head
    pads = jnp.stack([gap_start, head, gap_start + head, rest // ZERO_BIG,
                      rest % ZERO_BIG // ZERO_SMALL]).astype(jnp.int32)
    xs = _dispatch(x2d, pos0, pos1, pads, n_rows)
    ys = _moe_ffn(xs, tile_expert + li * N_EXPERTS, n_tiles, wg, wu, wd)
    return _combine(x2d, route, ys, pos0, pos1, g, b)


def _rope_tables(seq):
    inv_freq = ROPE_THETA ** (-jnp.arange(0, C_HEAD_DIM, 2, dtype=F32) / C_HEAD_DIM)
    ang = jnp.arange(seq, dtype=F32)[:, None] * inv_freq[None, :]
    cos, sin = jnp.cos(ang), jnp.sin(ang)
    cos_h = jnp.concatenate([cos, cos], axis=-1)
    sin_h = jnp.concatenate([-sin, sin], axis=-1)
    return jnp.tile(cos_h, (1, C_HEADS)), jnp.tile(sin_h, (1, C_HEADS))


def kernel(x, w_in, conv_w, a_ln_g, a_ln_b, a_ws, a_bs, b_a_log, b_dt_bias, b_norm_g, w_out, ln1_g, ln1_b, ln2_g, ln2_b, ffn_w_gate, ffn_w_up, ffn_w_down, moe_router, moe_w_gate, moe_w_up, moe_w_down):
    batch, seq, _ = x.shape
    T = batch * seq
    cos_t, sin_t = _rope_tables(seq)
    att_bias = _attn_bias(seq)
    o_a, o_bq, o_bz, o_beta, o_cq = 0, 2 * A_DIM, 2 * A_DIM + 3 * B_DIM, 2 * A_DIM + 4 * B_DIM, 2 * A_DIM + 4 * B_DIM + 2 * B_HEADS
    row = lambda v: v.reshape(1, -1)
    h2d = x.reshape(T, D_MODEL)
    dense_w = [t.astype(BF16) for t in (ffn_w_gate, ffn_w_up, ffn_w_down)]
    moe_w = [t.astype(BF16).reshape((-1,) + t.shape[2:]) for t in (moe_w_gate, moe_w_up, moe_w_down)]
    for layer in range(DEPTH):
        w = w_in[layer]
        wa = w[:, o_a:o_bq].astype(BF16)
        wb = w[:, o_bq:o_bz].astype(BF16)
        wz = w[:, o_bz:o_beta].astype(BF16)
        ws = _pack_narrow(w[:, o_beta:o_cq])
        wc = w[:, o_cq:].astype(BF16)
        au, av, bq, bk, bv, zs, small, cq, ck, cv = _inproj(h2d, wa, wb, wz, wc, ws, conv_w[layer], cos_t, sin_t, seq)

        bias2d = jnp.repeat(a_bs[layer].T, A_DIM // A_GROUPS, axis=1)
        ya = _gmlp(au, av, row(a_ln_g[layer]), row(a_ln_b[layer]), a_ws[layer], bias2d)

        zeros_row = jnp.zeros((1, LANES), F32)
        alog_row = zeros_row.at[0, B_HEADS:2 * B_HEADS].set(b_a_log[layer])
        dtb_row = zeros_row.at[0, B_HEADS:2 * B_HEADS].set(b_dt_bias[layer])
        u, wv, qd, kt, acomp, gl = _dn_prep(bq, bk, bv, small, alog_row, dtb_row)
        yb = _dn_scan(u, wv, qd, kt, acomp, gl, zs, row(b_norm_g[layer]), batch, seq)

        yc = _attention(cq, ck, cv, att_bias, batch, seq)

        wo = w_out[layer].astype(BF16)
        h2d = _outproj(h2d, ya, yb, yc, wo[:A_DIM], wo[A_DIM:A_DIM + B_DIM], wo[A_DIM + B_DIM:],
                       row(ln1_g[layer]), row(ln1_b[layer]))
        i = layer // 2
        if layer % 2 == 0:
            h2d = _ffn(h2d, *dense_w, i, row(ln2_g[layer]), row(ln2_b[layer]))
        else:
            h2d = _moe(h2d, moe_router[i], *moe_w, i, row(ln2_g[layer]), row(ln2_b[layer]))
    return h2d.reshape(batch, seq, D_MODEL)
```

```python
import functools

import jax
import jax.numpy as jnp
import numpy as np
from jax import lax
from jax.experimental import pallas as pl
from jax.experimental.pallas import tpu as pltpu

F32 = jnp.float32
BF16 = jnp.bfloat16

D_MODEL = 1024
DEPTH = 4
A_DIM = 256
A_GROUPS = 4
A_CHUNK = 128
B_HEAD_DIM = 128
B_DIM = 512
B_HEADS = 4
B_CONV = 4
B_CHUNK = 64
C_HEAD_DIM = 64
C_DIM = 256
C_HEADS = 4
C_CONFIGS = ((128, 1), (512, 4), (2048, 16))
ROPE_THETA = 10000.0
D_FF = 2816
N_EXPERTS = 8
DN_ALPHA = (2.0 * DEPTH) ** 0.25
LN_EPS = 1e-5
LOG2_E = 1.4426950408889634

LANES = 128
VMEM_LIMIT = 56 * 1024 * 1024
ROW_TILE = 512
ATT_BLOCK = 128
ATT_KEYS = 256
DN_BLOCK = 2 * B_CHUNK
DN_SCAN_SEQS = 2
DN_SCAN_ROWS = 1024
MOE_TILE = 512
FF_SUB = 256
DMA_ROWS = 512
DMA_UNROLL = 8
ZERO_BIG = 64
ZERO_SMALL = 8


def _params(*sem):
    return pltpu.CompilerParams(dimension_semantics=sem, vmem_limit_bytes=VMEM_LIMIT)


def _dot(a, b):
    return jnp.dot(a, b, preferred_element_type=F32)


def _dot_nt(a, b):
    return lax.dot_general(a, b, (((1,), (1,)), ((), ())), preferred_element_type=F32)


def _split3(x):
    x1 = x.astype(BF16)
    r = x - x1.astype(F32)
    x2 = r.astype(BF16)
    x3 = (r - x2.astype(F32)).astype(BF16)
    return x1, x2, x3


NARROW = 8


def _pack_narrow(w):
    pieces = _split3(w)
    packed = jnp.zeros((w.shape[0], LANES), BF16)
    for i, piece in enumerate(pieces):
        packed = packed.at[:, i * NARROW:(i + 1) * NARROW].set(piece)
    return packed


def _dot_narrow(x, w_packed):
    x1 = x.astype(BF16)
    x2 = (x - x1.astype(F32)).astype(BF16)
    p = _dot(x1, w_packed) + _dot(x2, w_packed)
    p = p + pltpu.roll(p, LANES - NARROW, 1) + pltpu.roll(p, LANES - 2 * NARROW, 1)
    lane = lax.broadcasted_iota(jnp.int32, p.shape, 1)
    return jnp.where(lane < NARROW, p, 0.0)


def _gelu_tanh(x):
    z = np.sqrt(2.0 / np.pi) * (x + 0.044715 * (x * x * x))
    return x * jax.nn.sigmoid(2.0 * z)


def _layer_norm(y, g, b):
    mu = jnp.mean(y, axis=-1, keepdims=True)
    yc = y - mu
    var = jnp.mean(yc * yc, axis=-1, keepdims=True)
    return yc * lax.rsqrt(var + LN_EPS) * g + b


def _const_spec(shape):
    nd = len(shape)
    return pl.BlockSpec(shape, lambda *_: (0,) * nd)


def _inproj_kernel(x_ref, xp_ref, wa_ref, wb_ref, wz_ref, wc_ref, ws_ref, cw_ref, cos_ref, sin_ref,
                   au_ref, av_ref, bq_ref, bk_ref, bv_ref, bz_ref, small_ref, cq_ref, ck_ref, cv_ref, *, spt):
    tm = x_ref.shape[0]
    x = x_ref[...]
    xb = x.astype(BF16)
    dk = B_HEAD_DIM
    seq_start = (pl.program_id(0) % spt) == 0
    xpb = jnp.where(seq_start, 0.0, xp_ref[...]).astype(BF16)

    def gmlp_out(a):
        a = _gelu_tanh(a)
        au_ref[...] = a[:, :A_DIM].astype(au_ref.dtype)
        av_ref[...] = a[:, A_DIM:].astype(av_ref.dtype)

    def conv_dots(part):
        w = wb_ref[:, part * B_DIM:(part + 1) * B_DIM]
        return _dot(xpb, w), _dot(xb, w)

    def conv_out(part, out_ref, dots):
        prev, cur = dots
        ext = jnp.concatenate([prev, cur], axis=0)
        cw = cw_ref[:, part * B_DIM:(part + 1) * B_DIM]
        acc = cur * cw[B_CONV - 1:B_CONV, :]
        for s in range(1, B_CONV):
            acc = acc + pltpu.roll(ext, s, 0)[8:, :] * cw[B_CONV - 1 - s:B_CONV - s, :]
        y = acc * jax.nn.sigmoid(acc)
        if part < 2:
            scale = dk ** -0.5 if part == 0 else 1.0
            segs = []
            for h in range(B_HEADS):
                seg = y[:, h * dk:(h + 1) * dk]
                segs.append(seg * (lax.rsqrt(jnp.sum(seg * seg, -1, keepdims=True) + 1e-6) * scale))
            y = jnp.concatenate(segs, axis=1)
        out_ref[...] = y.astype(out_ref.dtype)

    def gate_out(z):
        bz_ref[...] = (z * jax.nn.sigmoid(z)).astype(bz_ref.dtype)

    def rope_out(c):
        cos = cos_ref[...]
        sin = sin_ref[...]
        lane = lax.broadcasted_iota(jnp.int32, (tm, C_DIM), 1)
        first_half = (lane % C_HEAD_DIM) < (C_HEAD_DIM // 2)
        half = C_HEAD_DIM // 2

        def rope(t):
            swapped = jnp.where(first_half, pltpu.roll(t, C_DIM - half, 1), pltpu.roll(t, half, 1))
            return t * cos + swapped * sin

        cq_ref[...] = (rope(c[:, :C_DIM]) * (C_HEAD_DIM ** -0.5 * LOG2_E)).astype(BF16)
        ck_ref[...] = rope(c[:, C_DIM:2 * C_DIM]).astype(BF16)
        cv_ref[...] = c[:, 2 * C_DIM:].astype(BF16)

    stages = [(lambda: _dot(xb, wa_ref[...]), gmlp_out),
              (lambda: conv_dots(0), functools.partial(conv_out, 0, bq_ref)),
              (lambda: conv_dots(1), functools.partial(conv_out, 1, bk_ref)),
              (lambda: conv_dots(2), functools.partial(conv_out, 2, bv_ref)),
              (lambda: _dot(xb, wz_ref[...]), gate_out),
              (lambda: _dot(xb, wc_ref[...]), rope_out)]
    pending = stages[0][0]()
    for n, (_, epilogue) in enumerate(stages):
        current = pending
        if n + 1 < len(stages):
            pending = stages[n + 1][0]()
        epilogue(current)
    small_ref[...] = _dot_narrow(x, ws_ref[...])


def _inproj(x2d, wa, wb, wz, wc, ws, conv_w, cos_t, sin_t, seq):
    T = x2d.shape[0]
    tm = ROW_TILE
    spt = seq // tm
    row = lambda n: pl.BlockSpec((tm, n), lambda i: (i, 0))
    prev = pl.BlockSpec((8, D_MODEL), lambda i: (jnp.maximum(i * (tm // 8) - 1, 0), 0))
    pos = pl.BlockSpec((tm, C_DIM), lambda i: (i % spt, 0))
    outs = [(A_DIM, BF16), (A_DIM, BF16), (B_DIM, BF16), (B_DIM, BF16), (B_DIM, BF16), (B_DIM, BF16),
            (LANES, F32), (C_DIM, BF16), (C_DIM, BF16), (C_DIM, BF16)]
    return pl.pallas_call(
        functools.partial(_inproj_kernel, spt=spt),
        grid=(T // tm,),
        in_specs=[row(D_MODEL), prev, _const_spec(wa.shape), _const_spec(wb.shape), _const_spec(wz.shape),
                  _const_spec(wc.shape), _const_spec(ws.shape), _const_spec(conv_w.shape), pos, pos],
        out_specs=[row(n) for n, _ in outs],
        out_shape=[jax.ShapeDtypeStruct((T, n), dt) for n, dt in outs],
        compiler_params=_params("parallel"),
        name="inproj",
    )(x2d, x2d, wa, wb, wz, wc, ws, conv_w, cos_t, sin_t)


def _gmlp_kernel(u_ref, v_ref, g_ref, b_ref, ws_ref, bias_ref, o_ref):
    n = u_ref.shape[0] // A_CHUNK
    r = lax.broadcasted_iota(jnp.int32, (A_CHUNK, A_CHUNK), 0)
    c = lax.broadcasted_iota(jnp.int32, (A_CHUNK, A_CHUNK), 1)
    causal = r >= c
    group = lax.broadcasted_iota(jnp.int32, (A_CHUNK, A_DIM), 1) // (A_DIM // A_GROUPS)
    ws = [jnp.where(causal, ws_ref[g], 0.0).astype(BF16) for g in range(A_GROUPS)]
    bias = bias_ref[...]
    for i in range(n):
        rows = pl.ds(i * A_CHUNK, A_CHUNK)
        vn = _layer_norm(v_ref[rows, :].astype(F32), g_ref[...], b_ref[...]).astype(BF16)
        mixed = bias
        for g in range(A_GROUPS):
            mixed = mixed + jnp.where(group == g, _dot(ws[g], vn), 0.0)
        o_ref[rows, :] = (u_ref[rows, :].astype(F32) * mixed).astype(o_ref.dtype)


def _gmlp(au, av, ln_g, ln_b, ws, bias2d):
    T = au.shape[0]
    tm = ROW_TILE
    row = pl.BlockSpec((tm, A_DIM), lambda i: (i, 0))
    return pl.pallas_call(
        _gmlp_kernel,
        grid=(T // tm,),
        in_specs=[row, row, _const_spec(ln_g.shape), _const_spec(ln_b.shape),
                  _const_spec(ws.shape), _const_spec(bias2d.shape)],
        out_specs=row,
        out_shape=jax.ShapeDtypeStruct((T, A_DIM), BF16),
        compiler_params=_params("parallel"),
        name="gmlp",
    )(au, av, ln_g, ln_b, ws, bias2d)


def _unit_lower_inverse_minus_eye(ms, blk):
    def bf(t):
        return t.astype(BF16)

    xs = [-jnp.where(blk(16), m, 0.0) for m in ms]
    ps = xs
    ys = [_dot(bf(x), bf(x)) for x in xs]
    for step in range(3):
        ps = [p + y + _dot(bf(p), bf(y)) for p, y in zip(ps, ys)]
        if step < 2:
            ys = [_dot(bf(y), bf(y)) for y in ys]
    for size in (32, 64):
        off = blk(size) & jnp.logical_not(blk(size // 2))
        ls = [jnp.where(off, m, 0.0) for m in ms]
        qs = [l + _dot(bf(p), bf(l)) for p, l in zip(ps, ls)]
        ps = [p - (q + _dot(bf(q), bf(p))) for p, q in zip(ps, qs)]
    return ps


def _dn_prep_kernel(q_ref, k_ref, v_ref, s_ref, alog_ref, dtb_ref,
                    u_ref, w_ref, qd_ref, kt_ref, a_ref, gl_ref, *, heads):
    tm = q_ref.shape[0]
    dk = B_HEAD_DIM
    C = B_CHUNK
    hp = pl.program_id(1)

    r = lax.broadcasted_iota(jnp.int32, (DN_BLOCK, DN_BLOCK), 0)
    c = lax.broadcasted_iota(jnp.int32, (DN_BLOCK, DN_BLOCK), 1)

    def blk(size):
        return (r // size) == (c // size)

    lower = (r >= c) & blk(C)
    strict = (r > c) & blk(C)
    cum_mask = jnp.where(lower, 1.0, 0.0).astype(BF16)
    first_chunk = lax.broadcasted_iota(jnp.int32, (DN_BLOCK, dk), 0) < C
    lane = lax.broadcasted_iota(jnp.int32, (DN_BLOCK, LANES), 1)

    chains = [(b, h) for b in range(tm // DN_BLOCK) for h in range(heads)]
    rows = {b: slice(b * DN_BLOCK, (b + 1) * DN_BLOCK) for b, _ in chains}
    cols = {h: slice(h * dk, (h + 1) * dk) for _, h in chains}

    group = 2 * B_HEADS
    assert len(rows) * group <= LANES
    gates = {}
    packed = jnp.zeros((DN_BLOCK, LANES), F32)
    for b in rows:
        small = s_ref[rows[b], :]
        gates[b] = jax.nn.sigmoid(small)
        g_all = -jnp.exp(alog_ref[...]) * jax.nn.softplus(small + dtb_ref[...])
        packed = jnp.where(lane // group == b, g_all if b == 0 else pltpu.roll(g_all, group * b, 1), packed)
    g1, g2, g3 = _split3(packed)
    gc_all = _dot(cum_mask, g1) + _dot(cum_mask, g2) + _dot(cum_mask, g3)
    beta, gcol = [], []
    for b, h in chains:
        hh = hp * heads + h
        beta.append(jnp.sum(jnp.where(lane == hh, gates[b], 0.0), -1, keepdims=True))
        gc = jnp.sum(jnp.where(lane == group * b + B_HEADS + hh, gc_all, 0.0), -1, keepdims=True)
        gcol.append(jnp.broadcast_to(gc, (DN_BLOCK, dk)))

    q = [q_ref[rows[b], cols[h]].astype(F32) for b, h in chains]
    k = [k_ref[rows[b], cols[h]].astype(F32) for b, h in chains]
    kb = [ki * bi for ki, bi in zip(k, beta)]
    kk = [_dot_nt(kbi.astype(BF16), ki.astype(BF16)) for kbi, ki in zip(kb, k)]
    qk = [_dot_nt(qi.astype(BF16), ki.astype(BF16)) for qi, ki in zip(q, k)]
    decay = [jnp.exp(jnp.where(lower, gc - gc.T, -jnp.inf)) for gc in gcol]
    m = [jnp.where(strict, kki * d, 0.0) for kki, d in zip(kk, decay)]
    t_off = _unit_lower_inverse_minus_eye(m, blk)
    e_gc = [jnp.exp(gc) for gc in gcol]
    rhs = [jnp.concatenate([v_ref[rows[b], cols[h]].astype(F32) * bi, kbi * e], axis=1)
           for (b, h), bi, kbi, e in zip(chains, beta, kb, e_gc)]
    sol = [ri + _dot(t.astype(BF16), ri.astype(BF16)) for ri, t in zip(rhs, t_off)]

    for i, (b, h) in enumerate(chains):
        u_ref[rows[b], cols[h]] = sol[i][:, :dk].astype(BF16)
        w_ref[rows[b], cols[h]] = sol[i][:, dk:].astype(BF16)
        qd_ref[rows[b], cols[h]] = (q[i] * e_gc[i]).astype(BF16)
        glast = jnp.where(first_chunk, gcol[i][C - 1:C, :], gcol[i][2 * C - 1:2 * C, :])
        kt_ref[b, h] = (k[i] * jnp.exp(glast - gcol[i])).T.astype(BF16)
        attn = jnp.where(lower, qk[i] * decay[i], 0.0)
        a_ref[rows[b], h * C:(h + 1) * C] = (attn[:, :C] + attn[:, C:]).astype(BF16)
        gl = jnp.exp(glast)
        gl_ref[16 * b:16 * b + 8, cols[h]] = gl[0:8, :]
        gl_ref[16 * b + 8:16 * b + 16, cols[h]] = gl[C:C + 8, :]


def _dn_prep(bq, bk, bv, small, alog_row, dtb_row):
    T = bq.shape[0]
    heads = 2
    tm = 2 * ROW_TILE
    wcols = heads * B_HEAD_DIM
    nb = tm // DN_BLOCK
    qkv = pl.BlockSpec((tm, wcols), lambda i, h: (i, h))
    return pl.pallas_call(
        functools.partial(_dn_prep_kernel, heads=heads),
        grid=(T // tm, B_HEADS // heads),
        in_specs=[qkv, qkv, qkv, pl.BlockSpec((tm, LANES), lambda i, h: (i, 0)),
                  _const_spec(alog_row.shape), _const_spec(dtb_row.shape)],
        out_specs=[qkv, qkv, qkv,
                   pl.BlockSpec((nb, heads, B_HEAD_DIM, DN_BLOCK), lambda i, h: (i, h, 0, 0)),
                   pl.BlockSpec((tm, heads * B_CHUNK), lambda i, h: (i, h)),
                   pl.BlockSpec((8 * tm // B_CHUNK, wcols), lambda i, h: (i, h))],
        out_shape=[jax.ShapeDtypeStruct((T, B_DIM), BF16),
                   jax.ShapeDtypeStruct((T, B_DIM), BF16),
                   jax.ShapeDtypeStruct((T, B_DIM), BF16),
                   jax.ShapeDtypeStruct((T // DN_BLOCK, B_HEADS, B_HEAD_DIM, DN_BLOCK), BF16),
                   jax.ShapeDtypeStruct((T, B_HEADS * B_CHUNK), BF16),
                   jax.ShapeDtypeStruct((8 * T // B_CHUNK, B_DIM), F32)],
        compiler_params=_params("parallel", "parallel"),
        name="dn_prep",
    )(bq, bk, bv, small, alog_row, dtb_row)


def _dn_scan_kernel(u_ref, w_ref, qd_ref, kt_ref, a_ref, gl_ref, z_ref, ng_ref, o_ref, state_ref):
    nb, rows_per_step = u_ref.shape[0], u_ref.shape[1]
    dk = B_HEAD_DIM
    C = B_CHUNK

    @pl.when(pl.program_id(1) == 0)
    def _():
        state_ref[...] = jnp.zeros_like(state_ref)

    chains = [(s, h) for s in range(nb) for h in range(B_HEADS)]
    cols = [slice(h * dk, (h + 1) * dk) for h in range(B_HEADS)]

    def body(n, carry):
        for ci in range(2):
            rows = pl.ds(pl.multiple_of(n * DN_BLOCK + ci * C, C), C)
            glr = pl.ds(pl.multiple_of((2 * n + ci) * 8, 8), 8)
            states = [state_ref[s, h] for s, h in chains]
            r1 = [_dot(jnp.concatenate([w_ref[s, rows, cols[h]], qd_ref[s, rows, cols[h]]], axis=0),
                       st.astype(BF16)) for (s, h), st in zip(chains, states)]
            v_new = [(u_ref[s, rows, cols[h]].astype(F32) - r[:C]).astype(BF16) for (s, h), r in zip(chains, r1)]
            r2 = [_dot(jnp.concatenate([a_ref[s, rows, h * C:(h + 1) * C],
                                        kt_ref[s, n, h][:, ci * C:(ci + 1) * C]], axis=0), v)
                  for (s, h), v in zip(chains, v_new)]
            for i, (s, h) in enumerate(chains):
                state_ref[s, h] = states[i] * gl_ref[s, glr, cols[h]][0:1, :] + r2[i][C:]
                o = r1[i][C:] + r2[i][:C]
                o = o * lax.rsqrt(jnp.mean(o * o, -1, keepdims=True) + 1e-6) * ng_ref[...]
                o_ref[s, rows, cols[h]] = (o * z_ref[s, rows, cols[h]].astype(F32)).astype(o_ref.dtype)
        return carry

    lax.fori_loop(0, rows_per_step // DN_BLOCK, body, 0)


def _dn_scan(u, w, qd, kt, acomp, gl, zs, norm_g, batch, seq):
    nb = DN_SCAN_SEQS
    rows = DN_SCAN_ROWS

    def per_seq(arr, rows_of_block):
        arr = arr.reshape((batch, arr.shape[0] // batch) + arr.shape[1:])
        block = (nb, rows_of_block) + arr.shape[2:]
        return arr, pl.BlockSpec(block, lambda b, j: (b, j) + (0,) * (len(block) - 2))

    ins = [per_seq(u, rows), per_seq(w, rows), per_seq(qd, rows), per_seq(kt, rows // DN_BLOCK),
           per_seq(acomp, rows), per_seq(gl, 8 * rows // B_CHUNK), per_seq(zs, rows)]
    out = pl.pallas_call(
        _dn_scan_kernel,
        grid=(batch // nb, seq // rows),
        in_specs=[spec for _, spec in ins] + [_const_spec(norm_g.shape)],
        out_specs=pl.BlockSpec((nb, rows, B_DIM), lambda b, j: (b, j, 0)),
        out_shape=jax.ShapeDtypeStruct((batch, seq, B_DIM), BF16),
        scratch_shapes=[pltpu.VMEM((nb, B_HEADS, B_HEAD_DIM, B_HEAD_DIM), F32)],
        compiler_params=_params("parallel", "arbitrary"),
        name="dn_scan",
    )(*[arr for arr, _ in ins], norm_g)
    return out.reshape(batch * seq, B_DIM)


def _attn_kernel(q_ref, k_ref, v_ref, bias_ref, o_ref):
    S = q_ref.shape[0]
    width = q_ref.shape[1]
    heads = width // C_HEAD_DIM
    head_of_lane = lax.broadcasted_iota(jnp.int32, (ATT_BLOCK, width), 1) // C_HEAD_DIM

    def key_chunks(i):
        nk = (i + 1) * ATT_BLOCK
        return [(k0, min(ATT_KEYS, nk - k0)) for k0 in range(0, nk, ATT_KEYS)]

    def scores(i, h):
        nk = (i + 1) * ATT_BLOCK
        q = q_ref[pl.ds(i * ATT_BLOCK, ATT_BLOCK), :]
        qh = jnp.where(head_of_lane == h, q, jnp.zeros_like(q))
        return [_dot_nt(qh, k_ref[pl.ds(k0, kw), :]) + bias_ref[:, pl.ds(S - nk + k0, kw)]
                for k0, kw in key_chunks(i)]

    def over_keys(op, lane_reduce, parts):
        by_width = {}
        for t in parts:
            by_width[t.shape[1]] = op(by_width[t.shape[1]], t) if t.shape[1] in by_width else t
        return functools.reduce(op, [lane_reduce(t, axis=-1, keepdims=True) for t in by_width.values()])

    def softmax_pv(i, s):
        mx = over_keys(jnp.maximum, jnp.max, s)
        e = [jnp.exp2(t - mx) for t in s]
        den = over_keys(jnp.add, jnp.sum, e)
        acc = sum(_dot(t.astype(BF16), v_ref[pl.ds(k0, kw), :]) for t, (k0, kw) in zip(e, key_chunks(i)))
        return acc / den

    tasks = [(i, h) for i in range(S // ATT_BLOCK) for h in range(heads)]
    s_next = scores(*tasks[0])
    out = None
    for t, (i, h) in enumerate(tasks):
        s_cur = s_next
        if t + 1 < len(tasks):
            s_next = scores(*tasks[t + 1])
        pv = softmax_pv(i, s_cur)
        out = pv if h == 0 else jnp.where(head_of_lane == h, pv, out)
        if h == heads - 1:
            o_ref[pl.ds(i * ATT_BLOCK, ATT_BLOCK), :] = out.astype(o_ref.dtype)


def _attn_bias(seq):
    r = np.arange(ATT_BLOCK)[:, None]
    c = np.arange(seq)[None, :]
    dist = (seq - ATT_BLOCK) + r - c
    mult = np.zeros(dist.shape, np.float64)
    for window, dil in C_CONFIGS:
        mult += (dist >= 0) & (dist <= window) & (dist % dil == 0)
    with np.errstate(divide="ignore"):
        return jnp.asarray(np.log2(mult), F32)


def _attention(cq, ck, cv, bias, batch, seq):
    width = 2 * C_HEAD_DIM
    spec = pl.BlockSpec((seq, width), lambda b, p: (b, p))
    return pl.pallas_call(
        _attn_kernel,
        grid=(batch, C_DIM // width),
        in_specs=[spec, spec, spec, _const_spec(bias.shape)],
        out_specs=spec,
        out_shape=jax.ShapeDtypeStruct((batch * seq, C_DIM), BF16),
        compiler_params=_params("parallel", "parallel"),
        name="dilated_attn",
    )(cq, ck, cv, bias)


def _outproj_kernel(x_ref, ya_ref, yb_ref, yc_ref, wa_ref, wb_ref, wc_ref, g_ref, b_ref, o_ref):
    mix = _dot(ya_ref[...], wa_ref[...]) + _dot(yb_ref[...], wb_ref[...]) + _dot(yc_ref[...], wc_ref[...])
    o_ref[...] = _layer_norm(DN_ALPHA * x_ref[...] + mix, g_ref[...], b_ref[...])


def _outproj(x2d, ya, yb, yc, wa, wb, wc, g, b):
    T = x2d.shape[0]
    tm = ROW_TILE
    row = lambda n: pl.BlockSpec((tm, n), lambda i: (i, 0))
    return pl.pallas_call(
        _outproj_kernel,
        grid=(T // tm,),
        in_specs=[row(D_MODEL), row(A_DIM), row(B_DIM), row(C_DIM), _const_spec(wa.shape),
                  _const_spec(wb.shape), _const_spec(wc.shape), _const_spec(g.shape), _const_spec(b.shape)],
        out_specs=row(D_MODEL),
        out_shape=jax.ShapeDtypeStruct((T, D_MODEL), F32),
        compiler_params=_params("parallel"),
        name="outproj_ln",
    )(x2d, ya, yb, yc, wa, wb, wc, g, b)


def _swiglu(xb, wg_ref, wu_ref, wd_ref):
    slabs = [slice(c, c + FF_SUB) for c in range(0, D_FF, FF_SUB)]

    def gate_up(cols):
        return _dot(xb, wg_ref[0, :, cols]), _dot(xb, wu_ref[0, :, cols])

    acc = None
    nxt = gate_up(slabs[0])
    for c, cols in enumerate(slabs):
        gate, up = nxt
        if c + 1 < len(slabs):
            nxt = gate_up(slabs[c + 1])
        h = (gate * jax.nn.sigmoid(gate) * up).astype(BF16)
        part = _dot(h, wd_ref[0, cols, :])
        acc = part if acc is None else acc + part
    return acc


def _ffn_kernel(x_ref, wg_ref, wu_ref, wd_ref, g_ref, b_ref, o_ref):
    f = _swiglu(x_ref[...].astype(BF16), wg_ref, wu_ref, wd_ref)
    o_ref[...] = _layer_norm(DN_ALPHA * x_ref[...] + f, g_ref[...], b_ref[...])


def _ffn(x2d, wg, wu, wd, li, g, b):
    T = x2d.shape[0]
    tm = ROW_TILE
    return pl.pallas_call(
        _ffn_kernel,
        grid=(T // tm,),
        in_specs=[pl.BlockSpec((tm, D_MODEL), lambda i: (i, 0)),
                  pl.BlockSpec((1, D_MODEL, D_FF), lambda i: (li, 0, 0)),
                  pl.BlockSpec((1, D_MODEL, D_FF), lambda i: (li, 0, 0)),
                  pl.BlockSpec((1, D_FF, D_MODEL), lambda i: (li, 0, 0)),
                  _const_spec(g.shape), _const_spec(b.shape)],
        out_specs=pl.BlockSpec((tm, D_MODEL), lambda i: (i, 0)),
        out_shape=jax.ShapeDtypeStruct((T, D_MODEL), F32),
        compiler_params=_params("parallel"),
        name="ffn_ln",
    )(x2d, wg, wu, wd, g, b)


def _router_kernel(x_ref, w_ref, route_ref, counts_ref, carry_ref):
    i = pl.program_id(0)
    tm = x_ref.shape[0]

    @pl.when(i == 0)
    def _():
        carry_ref[...] = jnp.zeros_like(carry_ref)

    logits = _dot_narrow(x_ref[...], w_ref[...])
    lane = lax.broadcasted_iota(jnp.int32, (tm, LANES), 1)
    logits = jnp.where(lane < N_EXPERTS, logits, -jnp.inf)
    m1 = jnp.max(logits, axis=-1, keepdims=True)
    e1 = jnp.min(jnp.where(logits == m1, lane, LANES), axis=-1, keepdims=True)
    rest = jnp.where(lane == e1, -jnp.inf, logits)
    m2 = jnp.max(rest, axis=-1, keepdims=True)
    e2 = jnp.min(jnp.where(rest == m2, lane, LANES), axis=-1, keepdims=True)
    t = jnp.exp(m2 - m1)
    p1 = 1.0 / (1.0 + t)
    p2 = t / (1.0 + t)
    hot1 = lane == e1
    hot2 = lane == e2
    hot = jnp.where(hot1 | hot2, 1.0, 0.0)
    rr = lax.broadcasted_iota(jnp.int32, (tm, tm), 0)
    cc = lax.broadcasted_iota(jnp.int32, (tm, tm), 1)
    before = jnp.where(rr > cc, 1.0, 0.0).astype(BF16)
    cnt = _dot(before, hot.astype(BF16)) + carry_ref[0:1, :]
    rank1 = jnp.sum(jnp.where(hot1, cnt, 0.0), axis=-1, keepdims=True)
    rank2 = jnp.sum(jnp.where(hot2, cnt, 0.0), axis=-1, keepdims=True)
    route = jnp.where(lane == 0, e1.astype(F32), 0.0)
    route = jnp.where(lane == 1, e2.astype(F32), route)
    route = jnp.where(lane == 2, p1, route)
    route = jnp.where(lane == 3, p2, route)
    route = jnp.where(lane == 4, rank1, route)
    route = jnp.where(lane == 5, rank2, route)
    route_ref[...] = route
    carry_ref[...] = carry_ref[...] + jnp.sum(hot, axis=0, keepdims=True)
    counts_ref[...] = carry_ref[...]


def _router(x2d, w_pad):
    T = x2d.shape[0]
    tm = ROW_TILE
    return pl.pallas_call(
        _router_kernel,
        grid=(T // tm,),
        in_specs=[pl.BlockSpec((tm, D_MODEL), lambda i: (i, 0)), _const_spec(w_pad.shape)],
        out_specs=[pl.BlockSpec((tm, LANES), lambda i: (i, 0)), _const_spec((8, LANES))],
        out_shape=[jax.ShapeDtypeStruct((T, LANES), F32), jax.ShapeDtypeStruct((8, LANES), F32)],
        scratch_shapes=[pltpu.VMEM((8, LANES), F32)],
        compiler_params=_params("arbitrary"),
        name="moe_router",
    )(x2d, w_pad)


def _start_row_copies(n, make_copies):
    def step(g, c):
        for j in range(DMA_UNROLL):
            for idx, cp in enumerate(make_copies(g * DMA_UNROLL + j)):
                cp.start(priority=(j + idx) % 2)
        return c

    lax.fori_loop(0, n // DMA_UNROLL, step, 0)


def _dispatch_kernel(pos0_ref, pos1_ref, pad_ref, x_ref, xs_ref, zero_ref, sem, zero_sem):
    n = x_ref.shape[0]

    @pl.when(pl.program_id(0) == 0)
    def _():
        zero_ref[...] = jnp.zeros_like(zero_ref)
        for e in range(pad_ref.shape[1]):
            row, n_rows, row2, n_big, n_small = [pad_ref[k, e] for k in range(5)]

            def copies(kind, i, row=row, row2=row2, n_big=n_big):
                size = (1, ZERO_BIG, ZERO_SMALL)[kind]
                start = (row + i, row2 + i * ZERO_BIG, row2 + n_big * ZERO_BIG + i * ZERO_SMALL)[kind]
                if kind > 0:
                    start = pl.multiple_of(start, ZERO_SMALL)
                return pltpu.make_async_copy(zero_ref.at[pl.ds(0, size), :], xs_ref.at[pl.ds(start, size), :], zero_sem)

            for kind, count in enumerate((n_rows, n_big, n_small)):
                lax.fori_loop(0, count, lambda i, c, kind=kind: (copies(kind, i).start(), c)[1], 0)
            for kind, count in enumerate((n_rows, n_big, n_small)):
                lax.fori_loop(0, count, lambda i, c, kind=kind: (copies(kind, i).wait(), c)[1], 0)

    def copies(t):
        src = x_ref.at[pl.ds(t, 1), :]
        return (pltpu.make_async_copy(src, xs_ref.at[pl.ds(pos0_ref[0, 0, t], 1), :], sem),
                pltpu.make_async_copy(src, xs_ref.at[pl.ds(pos1_ref[0, 0, t], 1), :], sem))

    _start_row_copies(n, copies)
    for _ in range(2):
        pltpu.make_async_copy(x_ref, xs_ref.at[pl.ds(0, n), :], sem).wait()


def _dispatch(x2d, pos0, pos1, pads, n_rows):
    T = x2d.shape[0]
    td = DMA_ROWS
    idx = pl.BlockSpec((1, 1, td), lambda i: (i, 0, 0), memory_space=pltpu.SMEM)
    return pl.pallas_call(
        _dispatch_kernel,
        grid=(T // td,),
        in_specs=[idx, idx, pl.BlockSpec(memory_space=pltpu.SMEM),
                  pl.BlockSpec((td, D_MODEL), lambda i: (i, 0))],
        out_specs=pl.BlockSpec(memory_space=pl.ANY),
        out_shape=jax.ShapeDtypeStruct((n_rows, D_MODEL), F32),
        scratch_shapes=[pltpu.VMEM((ZERO_BIG, D_MODEL), F32), pltpu.SemaphoreType.DMA(()),
                        pltpu.SemaphoreType.DMA(())],
        compiler_params=_params("arbitrary"),
        name="moe_dispatch",
    )(pos0.reshape(T // td, 1, td), pos1.reshape(T // td, 1, td), pads, x2d)


def _combine_kernel(pos0_ref, pos1_ref, next0_ref, next1_ref, x_ref, route_ref, ys_ref, g_ref, b_ref, o_ref,
                    buf0, buf1, sems):
    i = pl.program_id(0)
    n = x_ref.shape[0]
    slot = i % 2

    def start_gather(p0_ref, p1_ref, dst):
        def copies(t):
            return (pltpu.make_async_copy(ys_ref.at[pl.ds(p0_ref[0, 0, t], 1), :],
                                          buf0.at[dst, pl.ds(t, 1), :], sems.at[dst]),
                    pltpu.make_async_copy(ys_ref.at[pl.ds(p1_ref[0, 0, t], 1), :],
                                          buf1.at[dst, pl.ds(t, 1), :], sems.at[dst]))

        _start_row_copies(n, copies)

    @pl.when(i == 0)
    def _():
        start_gather(pos0_ref, pos1_ref, 0)

    @pl.when(i + 1 < pl.num_programs(0))
    def _():
        start_gather(next0_ref, next1_ref, 1 - slot)

    for buf in (buf0, buf1):
        pltpu.make_async_copy(ys_ref.at[pl.ds(0, n), :], buf.at[slot], sems.at[slot]).wait()
    route = route_ref[...]
    f = route[:, 2:3] * buf0[slot] + route[:, 3:4] * buf1[slot]
    o_ref[...] = _layer_norm(DN_ALPHA * x_ref[...] + f, g_ref[...], b_ref[...])


def _combine(x2d, route, ys, pos0, pos1, g, b):
    T = x2d.shape[0]
    td = DMA_ROWS
    steps = T // td
    idx = pl.BlockSpec((1, 1, td), lambda i: (i, 0, 0), memory_space=pltpu.SMEM)
    nxt = pl.BlockSpec((1, 1, td), lambda i: (jnp.minimum(i + 1, steps - 1), 0, 0), memory_space=pltpu.SMEM)
    rows = [pos0.reshape(steps, 1, td), pos1.reshape(steps, 1, td)]
    buf = pltpu.VMEM((2, td, D_MODEL), F32)
    return pl.pallas_call(
        _combine_kernel,
        grid=(steps,),
        in_specs=[idx, idx, nxt, nxt, pl.BlockSpec((td, D_MODEL), lambda i: (i, 0)),
                  pl.BlockSpec((td, LANES), lambda i: (i, 0)),
                  pl.BlockSpec(memory_space=pl.ANY), _const_spec(g.shape), _const_spec(b.shape)],
        out_specs=pl.BlockSpec((td, D_MODEL), lambda i: (i, 0)),
        out_shape=jax.ShapeDtypeStruct((T, D_MODEL), F32),
        scratch_shapes=[buf, buf, pltpu.SemaphoreType.DMA((2,))],
        compiler_params=_params("arbitrary"),
        name="moe_combine_ln",
    )(*rows, *rows, x2d, route, ys, g, b)


def _moe_ffn_kernel(te_ref, nt_ref, x_ref, wg_ref, wu_ref, wd_ref, o_ref):
    del te_ref
    used = pl.program_id(0) < nt_ref[0]

    @pl.when(used)
    def _():
        o_ref[...] = _swiglu(x_ref[...].astype(BF16), wg_ref, wu_ref, wd_ref)

    @pl.when(jnp.logical_not(used))
    def _():
        o_ref[...] = jnp.zeros_like(o_ref)


def _moe_ffn(xs, tile_expert, n_tiles, wg, wu, wd):
    n_rows = xs.shape[0]
    tm = MOE_TILE

    def last_used(i, nt):
        return jnp.minimum(i, nt[0] - 1)

    def weights(shape):
        return pl.BlockSpec((1,) + shape, lambda i, te, nt: (te[last_used(i, nt)], 0, 0))

    grid_spec = pltpu.PrefetchScalarGridSpec(
        num_scalar_prefetch=2,
        grid=(n_rows // tm,),
        in_specs=[pl.BlockSpec((tm, D_MODEL), lambda i, te, nt: (last_used(i, nt), 0)),
                  weights((D_MODEL, D_FF)), weights((D_MODEL, D_FF)), weights((D_FF, D_MODEL))],
        out_specs=pl.BlockSpec((tm, D_MODEL), lambda i, te, nt: (i, 0)),
    )
    return pl.pallas_call(
        _moe_ffn_kernel,
        grid_spec=grid_spec,
        out_shape=jax.ShapeDtypeStruct((n_rows, D_MODEL), F32),
        compiler_params=_params("arbitrary"),
        name="moe_ffn",
    )(tile_expert, n_tiles, xs, wg, wu, wd)


def _moe(x2d, w_router, wg, wu, wd, li, g, b):
    T = x2d.shape[0]
    tm = MOE_TILE
    n_tiles_max = (2 * T) // tm + N_EXPERTS
    n_rows = n_tiles_max * tm
    route, counts = _router(x2d, _pack_narrow(w_router))
    cnt = counts[0, :N_EXPERTS].astype(jnp.int32)
    padded = ((cnt + tm - 1) // tm) * tm
    ends = jnp.cumsum(padded)
    offs = ends - padded
    e = route[:, 0:2].astype(jnp.int32)
    rank = route[:, 4:6].astype(jnp.int32)
    pos = offs[e] + rank
    pos0, pos1 = pos[:, 0], pos[:, 1]
    tile_start = jnp.arange(n_tiles_max, dtype=jnp.int32) * tm
    tile_expert = jnp.minimum(jnp.sum(tile_start[:, None] >= ends[None, :], axis=1), N_EXPERTS - 1).astype(jnp.int32)
    n_tiles = (ends[-1:] // tm).astype(jnp.int32)
    gap_start = jnp.append(offs + cnt, ends[-1])
    gap_len = jnp.append(padded - cnt, n_rows - ends[-1])
    head = jnp.minimum(-gap_start % ZERO_SMALL, gap_len)
    rest = gap_len - head
    pads = jnp.stack([gap_start, head, gap_start + head, rest // ZERO_BIG,
                      rest % ZERO_BIG // ZERO_SMALL]).astype(jnp.int32)
    xs = _dispatch(x2d, pos0, pos1, pads, n_rows)
    ys = _moe_ffn(xs, tile_expert + li * N_EXPERTS, n_tiles, wg, wu, wd)
    return _combine(x2d, route, ys, pos0, pos1, g, b)


def _rope_tables(seq):
    inv_freq = ROPE_THETA ** (-jnp.arange(0, C_HEAD_DIM, 2, dtype=F32) / C_HEAD_DIM)
    ang = jnp.arange(seq, dtype=F32)[:, None] * inv_freq[None, :]
    cos, sin = jnp.cos(ang), jnp.sin(ang)
    cos_h = jnp.concatenate([cos, cos], axis=-1)
    sin_h = jnp.concatenate([-sin, sin], axis=-1)
    return jnp.tile(cos_h, (1, C_HEADS)), jnp.tile(sin_h, (1, C_HEADS))


def kernel(x, w_in, conv_w, a_ln_g, a_ln_b, a_ws, a_bs, b_a_log, b_dt_bias, b_norm_g, w_out, ln1_g, ln1_b, ln2_g, ln2_b, ffn_w_gate, ffn_w_up, ffn_w_down, moe_router, moe_w_gate, moe_w_up, moe_w_down):
    batch, seq, _ = x.shape
    T = batch * seq
    cos_t, sin_t = _rope_tables(seq)
    att_bias = _attn_bias(seq)
    o_a, o_bq, o_bz, o_beta, o_cq = 0, 2 * A_DIM, 2 * A_DIM + 3 * B_DIM, 2 * A_DIM + 4 * B_DIM, 2 * A_DIM + 4 * B_DIM + 2 * B_HEADS
    row = lambda v: v.reshape(1, -1)
    h2d = x.reshape(T, D_MODEL)
    dense_w = [t.astype(BF16) for t in (ffn_w_gate, ffn_w_up, ffn_w_down)]
    moe_w = [t.astype(BF16).reshape((-1,) + t.shape[2:]) for t in (moe_w_gate, moe_w_up, moe_w_down)]
    for layer in range(DEPTH):
        w = w_in[layer]
        wa = w[:, o_a:o_bq].astype(BF16)
        wb = w[:, o_bq:o_bz].astype(BF16)
        wz = w[:, o_bz:o_beta].astype(BF16)
        ws = _pack_narrow(w[:, o_beta:o_cq])
        wc = w[:, o_cq:].astype(BF16)
        au, av, bq, bk, bv, zs, small, cq, ck, cv = _inproj(h2d, wa, wb, wz, wc, ws, conv_w[layer], cos_t, sin_t, seq)

        bias2d = jnp.repeat(a_bs[layer].T, A_DIM // A_GROUPS, axis=1)
        ya = _gmlp(au, av, row(a_ln_g[layer]), row(a_ln_b[layer]), a_ws[layer], bias2d)

        zeros_row = jnp.zeros((1, LANES), F32)
        alog_row = zeros_row.at[0, B_HEADS:2 * B_HEADS].set(b_a_log[layer])
        dtb_row = zeros_row.at[0, B_HEADS:2 * B_HEADS].set(b_dt_bias[layer])
        u, wv, qd, kt, acomp, gl = _dn_prep(bq, bk, bv, small, alog_row, dtb_row)
        yb = _dn_scan(u, wv, qd, kt, acomp, gl, zs, row(b_norm_g[layer]), batch, seq)

        yc = _attention(cq, ck, cv, att_bias, batch, seq)

        wo = w_out[layer].astype(BF16)
        h2d = _outproj(h2d, ya, yb, yc, wo[:A_DIM], wo[A_DIM:A_DIM + B_DIM], wo[A_DIM + B_DIM:],
                       row(ln1_g[layer]), row(ln1_b[layer]))
        i = layer // 2
        if layer % 2 == 0:
            h2d = _ffn(h2d, *dense_w, i, row(ln2_g[layer]), row(ln2_b[layer]))
        else:
            h2d = _moe(h2d, moe_router[i], *moe_w, i, row(ln2_g[layer]), row(ln2_b[layer]))
    return h2d.reshape(batch, seq, D_MODEL)
```

```python
import functools

import jax
import jax.numpy as jnp
import numpy as np
from jax import lax
from jax.experimental import pallas as pl
from jax.experimental.pallas import tpu as pltpu

F32 = jnp.float32
BF16 = jnp.bfloat16

D_MODEL = 1024
DEPTH = 4
A_DIM = 256
A_GROUPS = 4
A_CHUNK = 128
B_HEAD_DIM = 128
B_DIM = 512
B_HEADS = 4
B_CONV = 4
B_CHUNK = 64
C_HEAD_DIM = 64
C_DIM = 256
C_HEADS = 4
C_CONFIGS = ((128, 1), (512, 4), (2048, 16))
ROPE_THETA = 10000.0
D_FF = 2816
N_EXPERTS = 8
DN_ALPHA = (2.0 * DEPTH) ** 0.25
LN_EPS = 1e-5
LOG2_E = 1.4426950408889634

LANES = 128
VMEM_LIMIT = 56 * 1024 * 1024
ROW_TILE = 512
ATT_BLOCK = 128
ATT_KEYS = 256
DN_BLOCK = 2 * B_CHUNK
DN_SCAN_SEQS = 4
DN_SCAN_ROWS = 512
MOE_TILE = 512
FF_SUB = 256
DMA_ROWS = 512
DMA_UNROLL = 8
ZERO_BIG = 64
ZERO_SMALL = 8


def _params(*sem):
    return pltpu.CompilerParams(dimension_semantics=sem, vmem_limit_bytes=VMEM_LIMIT)


def _dot(a, b):
    return jnp.dot(a, b, preferred_element_type=F32)


def _dot_nt(a, b):
    return lax.dot_general(a, b, (((1,), (1,)), ((), ())), preferred_element_type=F32)


def _split3(x):
    x1 = x.astype(BF16)
    r = x - x1.astype(F32)
    x2 = r.astype(BF16)
    x3 = (r - x2.astype(F32)).astype(BF16)
    return x1, x2, x3


NARROW = 8


def _pack_narrow(w):
    pieces = _split3(w)
    packed = jnp.zeros((w.shape[0], LANES), BF16)
    for i, piece in enumerate(pieces):
        packed = packed.at[:, i * NARROW:(i + 1) * NARROW].set(piece)
    return packed


def _dot_narrow(x, w_packed):
    x1 = x.astype(BF16)
    x2 = (x - x1.astype(F32)).astype(BF16)
    p = _dot(x1, w_packed) + _dot(x2, w_packed)
    p = p + pltpu.roll(p, LANES - NARROW, 1) + pltpu.roll(p, LANES - 2 * NARROW, 1)
    lane = lax.broadcasted_iota(jnp.int32, p.shape, 1)
    return jnp.where(lane < NARROW, p, 0.0)


def _gelu_tanh(x):
    z = np.sqrt(2.0 / np.pi) * (x + 0.044715 * (x * x * x))
    return x * jax.nn.sigmoid(2.0 * z)


def _layer_norm(y, g, b):
    mu = jnp.mean(y, axis=-1, keepdims=True)
    yc = y - mu
    var = jnp.mean(yc * yc, axis=-1, keepdims=True)
    return yc * lax.rsqrt(var + LN_EPS) * g + b


def _const_spec(shape):
    nd = len(shape)
    return pl.BlockSpec(shape, lambda *_: (0,) * nd)


def _inproj_kernel(x_ref, xp_ref, wa_ref, wb_ref, wz_ref, wc_ref, ws_ref, cw_ref, cos_ref, sin_ref,
                   au_ref, av_ref, bq_ref, bk_ref, bv_ref, bz_ref, small_ref, cq_ref, ck_ref, cv_ref, *, spt):
    tm = x_ref.shape[0]
    x = x_ref[...]
    xb = x.astype(BF16)
    dk = B_HEAD_DIM
    seq_start = (pl.program_id(0) % spt) == 0
    xpb = jnp.where(seq_start, 0.0, xp_ref[...]).astype(BF16)

    def gmlp_out(a):
        a = _gelu_tanh(a)
        au_ref[...] = a[:, :A_DIM].astype(au_ref.dtype)
        av_ref[...] = a[:, A_DIM:].astype(av_ref.dtype)

    def conv_dots(part):
        w = wb_ref[:, part * B_DIM:(part + 1) * B_DIM]
        return _dot(xpb, w), _dot(xb, w)

    def conv_out(part, out_ref, dots):
        prev, cur = dots
        ext = jnp.concatenate([prev, cur], axis=0)
        cw = cw_ref[:, part * B_DIM:(part + 1) * B_DIM]
        acc = cur * cw[B_CONV - 1:B_CONV, :]
        for s in range(1, B_CONV):
            acc = acc + pltpu.roll(ext, s, 0)[8:, :] * cw[B_CONV - 1 - s:B_CONV - s, :]
        y = acc * jax.nn.sigmoid(acc)
        if part < 2:
            scale = dk ** -0.5 if part == 0 else 1.0
            segs = []
            for h in range(B_HEADS):
                seg = y[:, h * dk:(h + 1) * dk]
                segs.append(seg * (lax.rsqrt(jnp.sum(seg * seg, -1, keepdims=True) + 1e-6) * scale))
            y = jnp.concatenate(segs, axis=1)
        out_ref[...] = y.astype(out_ref.dtype)

    def gate_out(z):
        bz_ref[...] = (z * jax.nn.sigmoid(z)).astype(bz_ref.dtype)

    def rope_out(c):
        cos = cos_ref[...]
        sin = sin_ref[...]
        lane = lax.broadcasted_iota(jnp.int32, (tm, C_DIM), 1)
        first_half = (lane % C_HEAD_DIM) < (C_HEAD_DIM // 2)
        half = C_HEAD_DIM // 2

        def rope(t):
            swapped = jnp.where(first_half, pltpu.roll(t, C_DIM - half, 1), pltpu.roll(t, half, 1))
            return t * cos + swapped * sin

        cq_ref[...] = (rope(c[:, :C_DIM]) * (C_HEAD_DIM ** -0.5 * LOG2_E)).astype(BF16)
        ck_ref[...] = rope(c[:, C_DIM:2 * C_DIM]).astype(BF16)
        cv_ref[...] = c[:, 2 * C_DIM:].astype(BF16)

    stages = [(lambda: _dot(xb, wa_ref[...]), gmlp_out),
              (lambda: conv_dots(0), functools.partial(conv_out, 0, bq_ref)),
              (lambda: conv_dots(1), functools.partial(conv_out, 1, bk_ref)),
              (lambda: conv_dots(2), functools.partial(conv_out, 2, bv_ref)),
              (lambda: _dot(xb, wz_ref[...]), gate_out),
              (lambda: _dot(xb, wc_ref[...]), rope_out)]
    pending = stages[0][0]()
    for n, (_, epilogue) in enumerate(stages):
        current = pending
        if n + 1 < len(stages):
            pending = stages[n + 1][0]()
        epilogue(current)
    small_ref[...] = _dot_narrow(x, ws_ref[...])


def _inproj(x2d, wa, wb, wz, wc, ws, conv_w, cos_t, sin_t, seq):
    T = x2d.shape[0]
    tm = ROW_TILE
    spt = seq // tm
    row = lambda n: pl.BlockSpec((tm, n), lambda i: (i, 0))
    prev = pl.BlockSpec((8, D_MODEL), lambda i: (jnp.maximum(i * (tm // 8) - 1, 0), 0))
    pos = pl.BlockSpec((tm, C_DIM), lambda i: (i % spt, 0))
    outs = [(A_DIM, BF16), (A_DIM, BF16), (B_DIM, BF16), (B_DIM, BF16), (B_DIM, BF16), (B_DIM, BF16),
            (LANES, F32), (C_DIM, BF16), (C_DIM, BF16), (C_DIM, BF16)]
    return pl.pallas_call(
        functools.partial(_inproj_kernel, spt=spt),
        grid=(T // tm,),
        in_specs=[row(D_MODEL), prev, _const_spec(wa.shape), _const_spec(wb.shape), _const_spec(wz.shape),
                  _const_spec(wc.shape), _const_spec(ws.shape), _const_spec(conv_w.shape), pos, pos],
        out_specs=[row(n) for n, _ in outs],
        out_shape=[jax.ShapeDtypeStruct((T, n), dt) for n, dt in outs],
        compiler_params=_params("parallel"),
        name="inproj",
    )(x2d, x2d, wa, wb, wz, wc, ws, conv_w, cos_t, sin_t)


def _gmlp_kernel(u_ref, v_ref, g_ref, b_ref, ws_ref, bias_ref, o_ref):
    n = u_ref.shape[0] // A_CHUNK
    r = lax.broadcasted_iota(jnp.int32, (A_CHUNK, A_CHUNK), 0)
    c = lax.broadcasted_iota(jnp.int32, (A_CHUNK, A_CHUNK), 1)
    causal = r >= c
    group = lax.broadcasted_iota(jnp.int32, (A_CHUNK, A_DIM), 1) // (A_DIM // A_GROUPS)
    ws = [jnp.where(causal, ws_ref[g], 0.0).astype(BF16) for g in range(A_GROUPS)]
    bias = bias_ref[...]
    for i in range(n):
        rows = pl.ds(i * A_CHUNK, A_CHUNK)
        vn = _layer_norm(v_ref[rows, :].astype(F32), g_ref[...], b_ref[...]).astype(BF16)
        mixed = bias
        for g in range(A_GROUPS):
            mixed = mixed + jnp.where(group == g, _dot(ws[g], vn), 0.0)
        o_ref[rows, :] = (u_ref[rows, :].astype(F32) * mixed).astype(o_ref.dtype)


def _gmlp(au, av, ln_g, ln_b, ws, bias2d):
    T = au.shape[0]
    tm = ROW_TILE
    row = pl.BlockSpec((tm, A_DIM), lambda i: (i, 0))
    return pl.pallas_call(
        _gmlp_kernel,
        grid=(T // tm,),
        in_specs=[row, row, _const_spec(ln_g.shape), _const_spec(ln_b.shape),
                  _const_spec(ws.shape), _const_spec(bias2d.shape)],
        out_specs=row,
        out_shape=jax.ShapeDtypeStruct((T, A_DIM), BF16),
        compiler_params=_params("parallel"),
        name="gmlp",
    )(au, av, ln_g, ln_b, ws, bias2d)


def _unit_lower_inverse_minus_eye(ms, blk):
    def bf(t):
        return t.astype(BF16)

    xs = [-jnp.where(blk(16), m, 0.0) for m in ms]
    ps = xs
    ys = [_dot(bf(x), bf(x)) for x in xs]
    for step in range(3):
        ps = [p + y + _dot(bf(p), bf(y)) for p, y in zip(ps, ys)]
        if step < 2:
            ys = [_dot(bf(y), bf(y)) for y in ys]
    for size in (32, 64):
        off = blk(size) & jnp.logical_not(blk(size // 2))
        ls = [jnp.where(off, m, 0.0) for m in ms]
        qs = [l + _dot(bf(p), bf(l)) for p, l in zip(ps, ls)]
        ps = [p - (q + _dot(bf(q), bf(p))) for p, q in zip(ps, qs)]
    return ps


def _dn_prep_kernel(q_ref, k_ref, v_ref, s_ref, alog_ref, dtb_ref,
                    u_ref, w_ref, qd_ref, kt_ref, a_ref, gl_ref, *, heads):
    tm = q_ref.shape[0]
    dk = B_HEAD_DIM
    C = B_CHUNK
    hp = pl.program_id(1)

    r = lax.broadcasted_iota(jnp.int32, (DN_BLOCK, DN_BLOCK), 0)
    c = lax.broadcasted_iota(jnp.int32, (DN_BLOCK, DN_BLOCK), 1)

    def blk(size):
        return (r // size) == (c // size)

    lower = (r >= c) & blk(C)
    strict = (r > c) & blk(C)
    cum_mask = jnp.where(lower, 1.0, 0.0).astype(BF16)
    first_chunk = lax.broadcasted_iota(jnp.int32, (DN_BLOCK, dk), 0) < C
    lane = lax.broadcasted_iota(jnp.int32, (DN_BLOCK, LANES), 1)

    chains = [(b, h) for b in range(tm // DN_BLOCK) for h in range(heads)]
    rows = {b: slice(b * DN_BLOCK, (b + 1) * DN_BLOCK) for b, _ in chains}
    cols = {h: slice(h * dk, (h + 1) * dk) for _, h in chains}

    group = 2 * B_HEADS
    assert len(rows) * group <= LANES
    gates = {}
    packed = jnp.zeros((DN_BLOCK, LANES), F32)
    for b in rows:
        small = s_ref[rows[b], :]
        gates[b] = jax.nn.sigmoid(small)
        g_all = -jnp.exp(alog_ref[...]) * jax.nn.softplus(small + dtb_ref[...])
        packed = jnp.where(lane // group == b, g_all if b == 0 else pltpu.roll(g_all, group * b, 1), packed)
    g1, g2, g3 = _split3(packed)
    gc_all = _dot(cum_mask, g1) + _dot(cum_mask, g2) + _dot(cum_mask, g3)
    beta, gcol = [], []
    for b, h in chains:
        hh = hp * heads + h
        beta.append(jnp.sum(jnp.where(lane == hh, gates[b], 0.0), -1, keepdims=True))
        gc = jnp.sum(jnp.where(lane == group * b + B_HEADS + hh, gc_all, 0.0), -1, keepdims=True)
        gcol.append(jnp.broadcast_to(gc, (DN_BLOCK, dk)))

    q = [q_ref[rows[b], cols[h]].astype(F32) for b, h in chains]
    k = [k_ref[rows[b], cols[h]].astype(F32) for b, h in chains]
    kb = [ki * bi for ki, bi in zip(k, beta)]
    kk = [_dot_nt(kbi.astype(BF16), ki.astype(BF16)) for kbi, ki in zip(kb, k)]
    qk = [_dot_nt(qi.astype(BF16), ki.astype(BF16)) for qi, ki in zip(q, k)]
    decay = [jnp.exp(jnp.where(lower, gc - gc.T, -jnp.inf)) for gc in gcol]
    m = [jnp.where(strict, kki * d, 0.0) for kki, d in zip(kk, decay)]
    t_off = _unit_lower_inverse_minus_eye(m, blk)
    e_gc = [jnp.exp(gc) for gc in gcol]
    rhs = [jnp.concatenate([v_ref[rows[b], cols[h]].astype(F32) * bi, kbi * e], axis=1)
           for (b, h), bi, kbi, e in zip(chains, beta, kb, e_gc)]
    sol = [ri + _dot(t.astype(BF16), ri.astype(BF16)) for ri, t in zip(rhs, t_off)]

    for i, (b, h) in enumerate(chains):
        u_ref[rows[b], cols[h]] = sol[i][:, :dk].astype(BF16)
        w_ref[rows[b], cols[h]] = sol[i][:, dk:].astype(BF16)
        qd_ref[rows[b], cols[h]] = (q[i] * e_gc[i]).astype(BF16)
        glast = jnp.where(first_chunk, gcol[i][C - 1:C, :], gcol[i][2 * C - 1:2 * C, :])
        kt_ref[b, h] = (k[i] * jnp.exp(glast - gcol[i])).T.astype(BF16)
        attn = jnp.where(lower, qk[i] * decay[i], 0.0)
        a_ref[rows[b], h * C:(h + 1) * C] = (attn[:, :C] + attn[:, C:]).astype(BF16)
        gl = jnp.exp(glast)
        gl_ref[16 * b:16 * b + 8, cols[h]] = gl[0:8, :]
        gl_ref[16 * b + 8:16 * b + 16, cols[h]] = gl[C:C + 8, :]


def _dn_prep(bq, bk, bv, small, alog_row, dtb_row):
    T = bq.shape[0]
    heads = 2
    tm = 2 * ROW_TILE
    wcols = heads * B_HEAD_DIM
    nb = tm // DN_BLOCK
    qkv = pl.BlockSpec((tm, wcols), lambda i, h: (i, h))
    return pl.pallas_call(
        functools.partial(_dn_prep_kernel, heads=heads),
        grid=(T // tm, B_HEADS // heads),
        in_specs=[qkv, qkv, qkv, pl.BlockSpec((tm, LANES), lambda i, h: (i, 0)),
                  _const_spec(alog_row.shape), _const_spec(dtb_row.shape)],
        out_specs=[qkv, qkv, qkv,
                   pl.BlockSpec((nb, heads, B_HEAD_DIM, DN_BLOCK), lambda i, h: (i, h, 0, 0)),
                   pl.BlockSpec((tm, heads * B_CHUNK), lambda i, h: (i, h)),
                   pl.BlockSpec((8 * tm // B_CHUNK, wcols), lambda i, h: (i, h))],
        out_shape=[jax.ShapeDtypeStruct((T, B_DIM), BF16),
                   jax.ShapeDtypeStruct((T, B_DIM), BF16),
                   jax.ShapeDtypeStruct((T, B_DIM), BF16),
                   jax.ShapeDtypeStruct((T // DN_BLOCK, B_HEADS, B_HEAD_DIM, DN_BLOCK), BF16),
                   jax.ShapeDtypeStruct((T, B_HEADS * B_CHUNK), BF16),
                   jax.ShapeDtypeStruct((8 * T // B_CHUNK, B_DIM), F32)],
        compiler_params=_params("parallel", "parallel"),
        name="dn_prep",
    )(bq, bk, bv, small, alog_row, dtb_row)


def _dn_scan_kernel(u_ref, w_ref, qd_ref, kt_ref, a_ref, gl_ref, z_ref, ng_ref, o_ref, state_ref):
    nb, rows_per_step = u_ref.shape[0], u_ref.shape[1]
    dk = B_HEAD_DIM
    C = B_CHUNK

    @pl.when(pl.program_id(1) == 0)
    def _():
        state_ref[...] = jnp.zeros_like(state_ref)

    chains = [(s, h) for s in range(nb) for h in range(B_HEADS)]
    cols = [slice(h * dk, (h + 1) * dk) for h in range(B_HEADS)]

    def body(n, carry):
        for ci in range(2):
            rows = pl.ds(pl.multiple_of(n * DN_BLOCK + ci * C, C), C)
            glr = pl.ds(pl.multiple_of((2 * n + ci) * 8, 8), 8)
            states = [state_ref[s, h] for s, h in chains]
            r1 = [_dot(jnp.concatenate([w_ref[s, rows, cols[h]], qd_ref[s, rows, cols[h]]], axis=0),
                       st.astype(BF16)) for (s, h), st in zip(chains, states)]
            v_new = [(u_ref[s, rows, cols[h]].astype(F32) - r[:C]).astype(BF16) for (s, h), r in zip(chains, r1)]
            r2 = [_dot(jnp.concatenate([a_ref[s, rows, h * C:(h + 1) * C],
                                        kt_ref[s, n, h][:, ci * C:(ci + 1) * C]], axis=0), v)
                  for (s, h), v in zip(chains, v_new)]
            for i, (s, h) in enumerate(chains):
                state_ref[s, h] = states[i] * gl_ref[s, glr, cols[h]][0:1, :] + r2[i][C:]
                o = r1[i][C:] + r2[i][:C]
                o = o * lax.rsqrt(jnp.mean(o * o, -1, keepdims=True) + 1e-6) * ng_ref[...]
                o_ref[s, rows, cols[h]] = (o * z_ref[s, rows, cols[h]].astype(F32)).astype(o_ref.dtype)
        return carry

    lax.fori_loop(0, rows_per_step // DN_BLOCK, body, 0)


def _dn_scan(u, w, qd, kt, acomp, gl, zs, norm_g, batch, seq):
    nb = DN_SCAN_SEQS
    rows = DN_SCAN_ROWS

    def per_seq(arr, rows_of_block):
        arr = arr.reshape((batch, arr.shape[0] // batch) + arr.shape[1:])
        block = (nb, rows_of_block) + arr.shape[2:]
        return arr, pl.BlockSpec(block, lambda b, j: (b, j) + (0,) * (len(block) - 2))

    ins = [per_seq(u, rows), per_seq(w, rows), per_seq(qd, rows), per_seq(kt, rows // DN_BLOCK),
           per_seq(acomp, rows), per_seq(gl, 8 * rows // B_CHUNK), per_seq(zs, rows)]
    out = pl.pallas_call(
        _dn_scan_kernel,
        grid=(batch // nb, seq // rows),
        in_specs=[spec for _, spec in ins] + [_const_spec(norm_g.shape)],
        out_specs=pl.BlockSpec((nb, rows, B_DIM), lambda b, j: (b, j, 0)),
        out_shape=jax.ShapeDtypeStruct((batch, seq, B_DIM), BF16),
        scratch_shapes=[pltpu.VMEM((nb, B_HEADS, B_HEAD_DIM, B_HEAD_DIM), F32)],
        compiler_params=_params("parallel", "arbitrary"),
        name="dn_scan",
    )(*[arr for arr, _ in ins], norm_g)
    return out.reshape(batch * seq, B_DIM)


def _attn_kernel(q_ref, k_ref, v_ref, bias_ref, o_ref):
    S = q_ref.shape[0]
    width = q_ref.shape[1]
    heads = width // C_HEAD_DIM
    head_of_lane = lax.broadcasted_iota(jnp.int32, (ATT_BLOCK, width), 1) // C_HEAD_DIM

    def key_chunks(i):
        nk = (i + 1) * ATT_BLOCK
        return [(k0, min(ATT_KEYS, nk - k0)) for k0 in range(0, nk, ATT_KEYS)]

    def scores(i, h):
        nk = (i + 1) * ATT_BLOCK
        q = q_ref[pl.ds(i * ATT_BLOCK, ATT_BLOCK), :]
        qh = jnp.where(head_of_lane == h, q, jnp.zeros_like(q))
        return [_dot_nt(qh, k_ref[pl.ds(k0, kw), :]) + bias_ref[:, pl.ds(S - nk + k0, kw)]
                for k0, kw in key_chunks(i)]

    def over_keys(op, lane_reduce, parts):
        by_width = {}
        for t in parts:
            by_width[t.shape[1]] = op(by_width[t.shape[1]], t) if t.shape[1] in by_width else t
        return functools.reduce(op, [lane_reduce(t, axis=-1, keepdims=True) for t in by_width.values()])

    def softmax_pv(i, s):
        mx = over_keys(jnp.maximum, jnp.max, s)
        e = [jnp.exp2(t - mx) for t in s]
        den = over_keys(jnp.add, jnp.sum, e)
        acc = sum(_dot(t.astype(BF16), v_ref[pl.ds(k0, kw), :]) for t, (k0, kw) in zip(e, key_chunks(i)))
        return acc / den

    tasks = [(i, h) for i in range(S // ATT_BLOCK) for h in range(heads)]
    s_next = scores(*tasks[0])
    out = None
    for t, (i, h) in enumerate(tasks):
        s_cur = s_next
        if t + 1 < len(tasks):
            s_next = scores(*tasks[t + 1])
        pv = softmax_pv(i, s_cur)
        out = pv if h == 0 else jnp.where(head_of_lane == h, pv, out)
        if h == heads - 1:
            o_ref[pl.ds(i * ATT_BLOCK, ATT_BLOCK), :] = out.astype(o_ref.dtype)


def _attn_bias(seq):
    r = np.arange(ATT_BLOCK)[:, None]
    c = np.arange(seq)[None, :]
    dist = (seq - ATT_BLOCK) + r - c
    mult = np.zeros(dist.shape, np.float64)
    for window, dil in C_CONFIGS:
        mult += (dist >= 0) & (dist <= window) & (dist % dil == 0)
    with np.errstate(divide="ignore"):
        return jnp.asarray(np.log2(mult), F32)


def _attention(cq, ck, cv, bias, batch, seq):
    width = 2 * C_HEAD_DIM
    spec = pl.BlockSpec((seq, width), lambda b, p: (b, p))
    return pl.pallas_call(
        _attn_kernel,
        grid=(batch, C_DIM // width),
        in_specs=[spec, spec, spec, _const_spec(bias.shape)],
        out_specs=spec,
        out_shape=jax.ShapeDtypeStruct((batch * seq, C_DIM), BF16),
        compiler_params=_params("parallel", "parallel"),
        name="dilated_attn",
    )(cq, ck, cv, bias)


def _outproj_kernel(x_ref, ya_ref, yb_ref, yc_ref, wa_ref, wb_ref, wc_ref, g_ref, b_ref, o_ref):
    mix = _dot(ya_ref[...], wa_ref[...]) + _dot(yb_ref[...], wb_ref[...]) + _dot(yc_ref[...], wc_ref[...])
    o_ref[...] = _layer_norm(DN_ALPHA * x_ref[...] + mix, g_ref[...], b_ref[...])


def _outproj(x2d, ya, yb, yc, wa, wb, wc, g, b):
    T = x2d.shape[0]
    tm = ROW_TILE
    row = lambda n: pl.BlockSpec((tm, n), lambda i: (i, 0))
    return pl.pallas_call(
        _outproj_kernel,
        grid=(T // tm,),
        in_specs=[row(D_MODEL), row(A_DIM), row(B_DIM), row(C_DIM), _const_spec(wa.shape),
                  _const_spec(wb.shape), _const_spec(wc.shape), _const_spec(g.shape), _const_spec(b.shape)],
        out_specs=row(D_MODEL),
        out_shape=jax.ShapeDtypeStruct((T, D_MODEL), F32),
        compiler_params=_params("parallel"),
        name="outproj_ln",
    )(x2d, ya, yb, yc, wa, wb, wc, g, b)


def _swiglu(xb, wg_ref, wu_ref, wd_ref):
    slabs = [slice(c, c + FF_SUB) for c in range(0, D_FF, FF_SUB)]

    def gate_up(cols):
        return _dot(xb, wg_ref[0, :, cols]), _dot(xb, wu_ref[0, :, cols])

    acc = None
    nxt = gate_up(slabs[0])
    for c, cols in enumerate(slabs):
        gate, up = nxt
        if c + 1 < len(slabs):
            nxt = gate_up(slabs[c + 1])
        h = (gate * jax.nn.sigmoid(gate) * up).astype(BF16)
        part = _dot(h, wd_ref[0, cols, :])
        acc = part if acc is None else acc + part
    return acc


def _ffn_kernel(x_ref, wg_ref, wu_ref, wd_ref, g_ref, b_ref, o_ref):
    f = _swiglu(x_ref[...].astype(BF16), wg_ref, wu_ref, wd_ref)
    o_ref[...] = _layer_norm(DN_ALPHA * x_ref[...] + f, g_ref[...], b_ref[...])


def _ffn(x2d, wg, wu, wd, li, g, b):
    T = x2d.shape[0]
    tm = ROW_TILE
    return pl.pallas_call(
        _ffn_kernel,
        grid=(T // tm,),
        in_specs=[pl.BlockSpec((tm, D_MODEL), lambda i: (i, 0)),
                  pl.BlockSpec((1, D_MODEL, D_FF), lambda i: (li, 0, 0)),
                  pl.BlockSpec((1, D_MODEL, D_FF), lambda i: (li, 0, 0)),
                  pl.BlockSpec((1, D_FF, D_MODEL), lambda i: (li, 0, 0)),
                  _const_spec(g.shape), _const_spec(b.shape)],
        out_specs=pl.BlockSpec((tm, D_MODEL), lambda i: (i, 0)),
        out_shape=jax.ShapeDtypeStruct((T, D_MODEL), F32),
        compiler_params=_params("parallel"),
        name="ffn_ln",
    )(x2d, wg, wu, wd, g, b)


def _router_kernel(x_ref, w_ref, route_ref, counts_ref, carry_ref):
    i = pl.program_id(0)
    tm = x_ref.shape[0]

    @pl.when(i == 0)
    def _():
        carry_ref[...] = jnp.zeros_like(carry_ref)

    logits = _dot_narrow(x_ref[...], w_ref[...])
    lane = lax.broadcasted_iota(jnp.int32, (tm, LANES), 1)
    logits = jnp.where(lane < N_EXPERTS, logits, -jnp.inf)
    m1 = jnp.max(logits, axis=-1, keepdims=True)
    e1 = jnp.min(jnp.where(logits == m1, lane, LANES), axis=-1, keepdims=True)
    rest = jnp.where(lane == e1, -jnp.inf, logits)
    m2 = jnp.max(rest, axis=-1, keepdims=True)
    e2 = jnp.min(jnp.where(rest == m2, lane, LANES), axis=-1, keepdims=True)
    t = jnp.exp(m2 - m1)
    p1 = 1.0 / (1.0 + t)
    p2 = t / (1.0 + t)
    hot1 = lane == e1
    hot2 = lane == e2
    hot = jnp.where(hot1 | hot2, 1.0, 0.0)
    rr = lax.broadcasted_iota(jnp.int32, (tm, tm), 0)
    cc = lax.broadcasted_iota(jnp.int32, (tm, tm), 1)
    before = jnp.where(rr > cc, 1.0, 0.0).astype(BF16)
    cnt = _dot(before, hot.astype(BF16)) + carry_ref[0:1, :]
    rank1 = jnp.sum(jnp.where(hot1, cnt, 0.0), axis=-1, keepdims=True)
    rank2 = jnp.sum(jnp.where(hot2, cnt, 0.0), axis=-1, keepdims=True)
    route = jnp.where(lane == 0, e1.astype(F32), 0.0)
    route = jnp.where(lane == 1, e2.astype(F32), route)
    route = jnp.where(lane == 2, p1, route)
    route = jnp.where(lane == 3, p2, route)
    route = jnp.where(lane == 4, rank1, route)
    route = jnp.where(lane == 5, rank2, route)
    route_ref[...] = route
    carry_ref[...] = carry_ref[...] + jnp.sum(hot, axis=0, keepdims=True)
    counts_ref[...] = carry_ref[...]


def _router(x2d, w_pad):
    T = x2d.shape[0]
    tm = ROW_TILE
    return pl.pallas_call(
        _router_kernel,
        grid=(T // tm,),
        in_specs=[pl.BlockSpec((tm, D_MODEL), lambda i: (i, 0)), _const_spec(w_pad.shape)],
        out_specs=[pl.BlockSpec((tm, LANES), lambda i: (i, 0)), _const_spec((8, LANES))],
        out_shape=[jax.ShapeDtypeStruct((T, LANES), F32), jax.ShapeDtypeStruct((8, LANES), F32)],
        scratch_shapes=[pltpu.VMEM((8, LANES), F32)],
        compiler_params=_params("arbitrary"),
        name="moe_router",
    )(x2d, w_pad)


def _start_row_copies(n, make_copies):
    def step(g, c):
        for j in range(DMA_UNROLL):
            for idx, cp in enumerate(make_copies(g * DMA_UNROLL + j)):
                cp.start(priority=(j + idx) % 2)
        return c

    lax.fori_loop(0, n // DMA_UNROLL, step, 0)


def _dispatch_kernel(pos0_ref, pos1_ref, pad_ref, x_ref, xs_ref, zero_ref, sem, zero_sem):
    n = x_ref.shape[0]

    @pl.when(pl.program_id(0) == 0)
    def _():
        zero_ref[...] = jnp.zeros_like(zero_ref)
        for e in range(pad_ref.shape[1]):
            row, n_rows, row2, n_big, n_small = [pad_ref[k, e] for k in range(5)]

            def copies(kind, i, row=row, row2=row2, n_big=n_big):
                size = (1, ZERO_BIG, ZERO_SMALL)[kind]
                start = (row + i, row2 + i * ZERO_BIG, row2 + n_big * ZERO_BIG + i * ZERO_SMALL)[kind]
                if kind > 0:
                    start = pl.multiple_of(start, ZERO_SMALL)
                return pltpu.make_async_copy(zero_ref.at[pl.ds(0, size), :], xs_ref.at[pl.ds(start, size), :], zero_sem)

            for kind, count in enumerate((n_rows, n_big, n_small)):
                lax.fori_loop(0, count, lambda i, c, kind=kind: (copies(kind, i).start(), c)[1], 0)
            for kind, count in enumerate((n_rows, n_big, n_small)):
                lax.fori_loop(0, count, lambda i, c, kind=kind: (copies(kind, i).wait(), c)[1], 0)

    def copies(t):
        src = x_ref.at[pl.ds(t, 1), :]
        return (pltpu.make_async_copy(src, xs_ref.at[pl.ds(pos0_ref[0, 0, t], 1), :], sem),
                pltpu.make_async_copy(src, xs_ref.at[pl.ds(pos1_ref[0, 0, t], 1), :], sem))

    _start_row_copies(n, copies)
    for _ in range(2):
        pltpu.make_async_copy(x_ref, xs_ref.at[pl.ds(0, n), :], sem).wait()


def _dispatch(x2d, pos0, pos1, pads, n_rows):
    T = x2d.shape[0]
    td = DMA_ROWS
    idx = pl.BlockSpec((1, 1, td), lambda i: (i, 0, 0), memory_space=pltpu.SMEM)
    return pl.pallas_call(
        _dispatch_kernel,
        grid=(T // td,),
        in_specs=[idx, idx, pl.BlockSpec(memory_space=pltpu.SMEM),
                  pl.BlockSpec((td, D_MODEL), lambda i: (i, 0))],
        out_specs=pl.BlockSpec(memory_space=pl.ANY),
        out_shape=jax.ShapeDtypeStruct((n_rows, D_MODEL), F32),
        scratch_shapes=[pltpu.VMEM((ZERO_BIG, D_MODEL), F32), pltpu.SemaphoreType.DMA(()),
                        pltpu.SemaphoreType.DMA(())],
        compiler_params=_params("arbitrary"),
        name="moe_dispatch",
    )(pos0.reshape(T // td, 1, td), pos1.reshape(T // td, 1, td), pads, x2d)


def _combine_kernel(pos0_ref, pos1_ref, next0_ref, next1_ref, x_ref, route_ref, ys_ref, g_ref, b_ref, o_ref,
                    buf0, buf1, sems):
    i = pl.program_id(0)
    n = x_ref.shape[0]
    slot = i % 2

    def start_gather(p0_ref, p1_ref, dst):
        def copies(t):
            return (pltpu.make_async_copy(ys_ref.at[pl.ds(p0_ref[0, 0, t], 1), :],
                                          buf0.at[dst, pl.ds(t, 1), :], sems.at[dst]),
                    pltpu.make_async_copy(ys_ref.at[pl.ds(p1_ref[0, 0, t], 1), :],
                                          buf1.at[dst, pl.ds(t, 1), :], sems.at[dst]))

        _start_row_copies(n, copies)

    @pl.when(i == 0)
    def _():
        start_gather(pos0_ref, pos1_ref, 0)

    @pl.when(i + 1 < pl.num_programs(0))
    def _():
        start_gather(next0_ref, next1_ref, 1 - slot)

    for buf in (buf0, buf1):
        pltpu.make_async_copy(ys_ref.at[pl.ds(0, n), :], buf.at[slot], sems.at[slot]).wait()
    route = route_ref[...]
    f = route[:, 2:3] * buf0[slot] + route[:, 3:4] * buf1[slot]
    o_ref[...] = _layer_norm(DN_ALPHA * x_ref[...] + f, g_ref[...], b_ref[...])


def _combine(x2d, route, ys, pos0, pos1, g, b):
    T = x2d.shape[0]
    td = DMA_ROWS
    steps = T // td
    idx = pl.BlockSpec((1, 1, td), lambda i: (i, 0, 0), memory_space=pltpu.SMEM)
    nxt = pl.BlockSpec((1, 1, td), lambda i: (jnp.minimum(i + 1, steps - 1), 0, 0), memory_space=pltpu.SMEM)
    rows = [pos0.reshape(steps, 1, td), pos1.reshape(steps, 1, td)]
    buf = pltpu.VMEM((2, td, D_MODEL), F32)
    return pl.pallas_call(
        _combine_kernel,
        grid=(steps,),
        in_specs=[idx, idx, nxt, nxt, pl.BlockSpec((td, D_MODEL), lambda i: (i, 0)),
                  pl.BlockSpec((td, LANES), lambda i: (i, 0)),
                  pl.BlockSpec(memory_space=pl.ANY), _const_spec(g.shape), _const_spec(b.shape)],
        out_specs=pl.BlockSpec((td, D_MODEL), lambda i: (i, 0)),
        out_shape=jax.ShapeDtypeStruct((T, D_MODEL), F32),
        scratch_shapes=[buf, buf, pltpu.SemaphoreType.DMA((2,))],
        compiler_params=_params("arbitrary"),
        name="moe_combine_ln",
    )(*rows, *rows, x2d, route, ys, g, b)


def _moe_ffn_kernel(te_ref, nt_ref, x_ref, wg_ref, wu_ref, wd_ref, o_ref):
    del te_ref
    used = pl.program_id(0) < nt_ref[0]

    @pl.when(used)
    def _():
        o_ref[...] = _swiglu(x_ref[...].astype(BF16), wg_ref, wu_ref, wd_ref)

    @pl.when(jnp.logical_not(used))
    def _():
        o_ref[...] = jnp.zeros_like(o_ref)


def _moe_ffn(xs, tile_expert, n_tiles, wg, wu, wd):
    n_rows = xs.shape[0]
    tm = MOE_TILE

    def last_used(i, nt):
        return jnp.minimum(i, nt[0] - 1)

    def weights(shape):
        return pl.BlockSpec((1,) + shape, lambda i, te, nt: (te[last_used(i, nt)], 0, 0))

    grid_spec = pltpu.PrefetchScalarGridSpec(
        num_scalar_prefetch=2,
        grid=(n_rows // tm,),
        in_specs=[pl.BlockSpec((tm, D_MODEL), lambda i, te, nt: (last_used(i, nt), 0)),
                  weights((D_MODEL, D_FF)), weights((D_MODEL, D_FF)), weights((D_FF, D_MODEL))],
        out_specs=pl.BlockSpec((tm, D_MODEL), lambda i, te, nt: (i, 0)),
    )
    return pl.pallas_call(
        _moe_ffn_kernel,
        grid_spec=grid_spec,
        out_shape=jax.ShapeDtypeStruct((n_rows, D_MODEL), F32),
        compiler_params=_params("arbitrary"),
        name="moe_ffn",
    )(tile_expert, n_tiles, xs, wg, wu, wd)


def _moe(x2d, w_router, wg, wu, wd, li, g, b):
    T = x2d.shape[0]
    tm = MOE_TILE
    n_tiles_max = (2 * T) // tm + N_EXPERTS
    n_rows = n_tiles_max * tm
    route, counts = _router(x2d, _pack_narrow(w_router))
    cnt = counts[0, :N_EXPERTS].astype(jnp.int32)
    padded = ((cnt + tm - 1) // tm) * tm
    ends = jnp.cumsum(padded)
    offs = ends - padded
    e = route[:, 0:2].astype(jnp.int32)
    rank = route[:, 4:6].astype(jnp.int32)
    pos = offs[e] + rank
    pos0, pos1 = pos[:, 0], pos[:, 1]
    tile_start = jnp.arange(n_tiles_max, dtype=jnp.int32) * tm
    tile_expert = jnp.minimum(jnp.sum(tile_start[:, None] >= ends[None, :], axis=1), N_EXPERTS - 1).astype(jnp.int32)
    n_tiles = (ends[-1:] // tm).astype(jnp.int32)
    gap_start = jnp.append(offs + cnt, ends[-1])
    gap_len = jnp.append(padded - cnt, n_rows - ends[-1])
    head = jnp.minimum(-gap_start % ZERO_SMALL, gap_len)
    rest = gap_len - head
    pads = jnp.stack([gap_start, head, gap_start + head, rest // ZERO_BIG,
                      rest % ZERO_BIG // ZERO_SMALL]).astype(jnp.int32)
    xs = _dispatch(x2d, pos0, pos1, pads, n_rows)
    ys = _moe_ffn(xs, tile_expert + li * N_EXPERTS, n_tiles, wg, wu, wd)
    return _combine(x2d, route, ys, pos0, pos1, g, b)


def _rope_tables(seq):
    inv_freq = ROPE_THETA ** (-jnp.arange(0, C_HEAD_DIM, 2, dtype=F32) / C_HEAD_DIM)
    ang = jnp.arange(seq, dtype=F32)[:, None] * inv_freq[None, :]
    cos, sin = jnp.cos(ang), jnp.sin(ang)
    cos_h = jnp.concatenate([cos, cos], axis=-1)
    sin_h = jnp.concatenate([-sin, sin], axis=-1)
    return jnp.tile(cos_h, (1, C_HEADS)), jnp.tile(sin_h, (1, C_HEADS))


def kernel(x, w_in, conv_w, a_ln_g, a_ln_b, a_ws, a_bs, b_a_log, b_dt_bias, b_norm_g, w_out, ln1_g, ln1_b, ln2_g, ln2_b, ffn_w_gate, ffn_w_up, ffn_w_down, moe_router, moe_w_gate, moe_w_up, moe_w_down):
    batch, seq, _ = x.shape
    T = batch * seq
    cos_t, sin_t = _rope_tables(seq)
    att_bias = _attn_bias(seq)
    o_a, o_bq, o_bz, o_beta, o_cq = 0, 2 * A_DIM, 2 * A_DIM + 3 * B_DIM, 2 * A_DIM + 4 * B_DIM, 2 * A_DIM + 4 * B_DIM + 2 * B_HEADS
    row = lambda v: v.reshape(1, -1)
    h2d = x.reshape(T, D_MODEL)
    dense_w = [t.astype(BF16) for t in (ffn_w_gate, ffn_w_up, ffn_w_down)]
    moe_w = [t.astype(BF16).reshape((-1,) + t.shape[2:]) for t in (moe_w_gate, moe_w_up, moe_w_down)]
    for layer in range(DEPTH):
        w = w_in[layer]
        wa = w[:, o_a:o_bq].astype(BF16)
        wb = w[:, o_bq:o_bz].astype(BF16)
        wz = w[:, o_bz:o_beta].astype(BF16)
        ws = _pack_narrow(w[:, o_beta:o_cq])
        wc = w[:, o_cq:].astype(BF16)
        au, av, bq, bk, bv, zs, small, cq, ck, cv = _inproj(h2d, wa, wb, wz, wc, ws, conv_w[layer], cos_t, sin_t, seq)

        bias2d = jnp.repeat(a_bs[layer].T, A_DIM // A_GROUPS, axis=1)
        ya = _gmlp(au, av, row(a_ln_g[layer]), row(a_ln_b[layer]), a_ws[layer], bias2d)

        zeros_row = jnp.zeros((1, LANES), F32)
        alog_row = zeros_row.at[0, B_HEADS:2 * B_HEADS].set(b_a_log[layer])
        dtb_row = zeros_row.at[0, B_HEADS:2 * B_HEADS].set(b_dt_bias[layer])
        u, wv, qd, kt, acomp, gl = _dn_prep(bq, bk, bv, small, alog_row, dtb_row)
        yb = _dn_scan(u, wv, qd, kt, acomp, gl, zs, row(b_norm_g[layer]), batch, seq)

        yc = _attention(cq, ck, cv, att_bias, batch, seq)

        wo = w_out[layer].astype(BF16)
        h2d = _outproj(h2d, ya, yb, yc, wo[:A_DIM], wo[A_DIM:A_DIM + B_DIM], wo[A_DIM + B_DIM:],
                       row(ln1_g[layer]), row(ln1_b[layer]))
        i = layer // 2
        if layer % 2 == 0:
            h2d = _ffn(h2d, *dense_w, i, row(ln2_g[layer]), row(ln2_b[layer]))
        else:
            h2d = _moe(h2d, moe_router[i], *moe_w, i, row(ln2_g[layer]), row(ln2_b[layer]))
    return h2d.reshape(batch, seq, D_MODEL)
```
